```python
import jax
import jax.numpy as jnp
from jax import lax
import numpy as np

D_MODEL = 2048
BATCH = 2
SEQ = 16384
DEPTH = 2
DEC_BATCH = 32
DEC_SEQ = 64
PAST_LEN = 1024

CHUNK = 64
N_RET_HEADS = 8
RET_DK = 256
RET_DV = 512
RET_QK = N_RET_HEADS * RET_DK
RET_V = N_RET_HEADS * RET_DV
ROPE_BASE = 10000.0
N_ATT_HEADS = 16
ATT_HD = 128
ATT_W = N_ATT_HEADS * ATT_HD
BAND_CHUNKS = 8
ATT_BAND = BAND_CHUNKS * CHUNK
MAX_REL = 256
IN_COLS = 2 * RET_QK + 2 * RET_V + 3 * ATT_W + 2 * D_MODEL
N_GROUPS = 4
EXPERTS_PER_GROUP = 8
N_EXPERTS = N_GROUPS * EXPERTS_PER_GROUP
TOP_K = 2
EXPERT_FF = 1024
MOE_BLOCK = 128
EPS = 1e-6
NEG_INF = -1e30

kernel_name = 'chunk_stream_hybrid_step'


def in_split_points():
    sizes = [RET_QK, RET_QK, RET_V, RET_V, ATT_W, ATT_W, ATT_W, D_MODEL]
    pts, acc = [], 0
    for s in sizes:
        acc += s
        pts.append(acc)
    return pts


def rms_norm(x, w):
    xf = x.astype(jnp.float32)
    return (xf * lax.rsqrt(jnp.mean(xf * xf, axis=-1, keepdims=True) + EPS)).astype(x.dtype) * w


def head_rms_norm(t, w):
    tf = t.astype(jnp.float32)
    return (tf * lax.rsqrt(jnp.mean(tf * tf, axis=-1, keepdims=True) + EPS)).astype(t.dtype) * w


def head_group_norm(o, w):
    mu = jnp.mean(o, axis=-1, keepdims=True)
    var = jnp.mean(jnp.square(o - mu), axis=-1, keepdims=True)
    return (o - mu) * lax.rsqrt(var + EPS) * w.astype(jnp.float32)


def rotary(t, pos):
    half = t.shape[-1] // 2
    freqs = ROPE_BASE ** (-jnp.arange(half, dtype=jnp.float32) / half)
    ang = pos.astype(jnp.float32)[:, None] * freqs[None, :]
    cos = jnp.cos(ang)[None, :, None, :]
    sin = jnp.sin(ang)[None, :, None, :]
    t1 = t[..., :half].astype(jnp.float32)
    t2 = t[..., half:].astype(jnp.float32)
    return jnp.concatenate([t1 * cos - t2 * sin, t1 * sin + t2 * cos], axis=-1)


def retention(q, k, v, s0):
    B, S, H, _ = q.shape
    C = min(CHUNK, S)
    N = S // C
    log_gamma = jnp.log1p(-jnp.exp2(-5.0 - jnp.arange(H, dtype=jnp.float32)))
    idx = jnp.arange(C, dtype=jnp.float32)
    diff = idx[:, None] - idx[None, :]
    intra_decay = jnp.where(diff >= 0, jnp.exp(log_gamma[:, None, None] * jnp.maximum(diff, 0.0)), 0.0)
    cross_decay = jnp.exp(log_gamma[:, None] * (idx + 1.0))[:, :, None]
    state_decay = jnp.exp(log_gamma[:, None] * (C - 1.0 - idx))[:, :, None]
    chunk_decay = jnp.exp(log_gamma * C)[:, None, None]

    def to_chunks(t):
        return t.astype(jnp.float32).reshape(B, N, C, H, -1).transpose(1, 0, 3, 2, 4)

    def step(s, qkv):
        qc, kc, vc = qkv
        scores = jnp.einsum('bhnd,bhmd->bhnm', qc, kc) * intra_decay
        o = jnp.einsum('bhnm,bhmv->bhnv', scores, vc) + jnp.einsum('bhnd,bhdv->bhnv', qc, s) * cross_decay
        s = chunk_decay * s + jnp.einsum('bhmd,bhmv->bhdv', kc * state_decay, vc)
        return s, o

    s_final, o = lax.scan(step, s0.astype(jnp.float32), (to_chunks(q), to_chunks(k), to_chunks(v)))
    return o.transpose(1, 0, 3, 2, 4).reshape(B, S, H, -1), s_final


def chunk_band_attention(q, k, v, k_past, v_past, n_past_valid, rel_table):
    B, S, H, hd = q.shape
    R = k_past.shape[1]
    C = min(CHUNK, S)
    N = S // C
    k_all = jnp.concatenate([k_past.astype(k.dtype), k], axis=1)
    v_all = jnp.concatenate([v_past.astype(v.dtype), v], axis=1)
    a = jnp.arange(C)
    b = jnp.arange(R + C)
    rel = a[:, None] - (b[None, :] - R)
    bias = rel_table[:, jnp.clip(rel, -MAX_REL, MAX_REL) + MAX_REL].astype(jnp.float32)
    q_chunks = q.reshape(B, N, C, H, hd).transpose(1, 0, 2, 3, 4)
    scale = hd ** -0.5

    def one_chunk(args):
        n, qn = args
        kb = lax.dynamic_slice_in_dim(k_all, n * C, R + C, axis=1)
        vb = lax.dynamic_slice_in_dim(v_all, n * C, R + C, axis=1)
        valid = b >= (R - n_past_valid - n * C)
        s = jnp.einsum('bqhd,bkhd->bhqk', qn, kb).astype(jnp.float32) * scale + bias
        s = jnp.where(valid[None, None, None, :], s, NEG_INF)
        p = jax.nn.softmax(s, axis=-1)
        return jnp.einsum('bhqk,bkhd->bqhd', p.astype(vb.dtype), vb)

    o = lax.map(one_chunk, (jnp.arange(N), q_chunks))
    return o.transpose(1, 0, 2, 3, 4).reshape(B, S, H * hd)


def hier_moe(h, w_router_g, b_router_g, w_router_e, b_router_e, w_up, w_down):
    shp = h.shape
    hf = h.reshape(-1, D_MODEL)
    T = hf.shape[0]
    g_logits = (hf @ w_router_g).astype(jnp.float32) + b_router_g.astype(jnp.float32)
    g_prob = jax.nn.softmax(g_logits, axis=-1)
    g_sel = jnp.argmax(g_logits, axis=-1)
    p_g = jnp.take_along_axis(g_prob, g_sel[:, None], axis=-1)
    e_logits = ((hf @ w_router_e).astype(jnp.float32) + b_router_e.astype(jnp.float32)).reshape(T, N_GROUPS, EXPERTS_PER_GROUP)
    e_in_group = jnp.take_along_axis(e_logits, g_sel[:, None, None], axis=1)[:, 0]
    top_v, top_i = lax.top_k(e_in_group, TOP_K)
    gate = p_g * jax.nn.softmax(top_v, axis=-1)
    expert = g_sel[:, None] * EXPERTS_PER_GROUP + top_i

    A = T * TOP_K
    e_flat = expert.reshape(-1)
    tok_flat = jnp.repeat(jnp.arange(T), TOP_K)
    gate_flat = gate.reshape(-1)
    order = jnp.argsort(e_flat)
    e_s, tok_s, gate_s = e_flat[order], tok_flat[order], gate_flat[order]
    counts = jnp.bincount(e_flat, length=N_EXPERTS)
    starts = jnp.cumsum(counts) - counts
    pcounts = (counts + MOE_BLOCK - 1) // MOE_BLOCK * MOE_BLOCK
    pends = jnp.cumsum(pcounts)
    pstarts = pends - pcounts
    slot = pstarts[e_s] + (jnp.arange(A) - starts[e_s])
    NB = (A + N_EXPERTS * (MOE_BLOCK - 1) + MOE_BLOCK - 1) // MOE_BLOCK
    P = NB * MOE_BLOCK
    buf = jnp.zeros((P, D_MODEL), h.dtype).at[slot].set(hf[tok_s])
    block_e = jnp.minimum(jnp.searchsorted(pends, jnp.arange(NB) * MOE_BLOCK, side='right'), N_EXPERTS - 1)

    def expert_block(args):
        xb, e = args
        up = xb @ w_up[e]
        a_, b_ = jnp.split(up, 2, axis=-1)
        return (jax.nn.silu(a_) * b_) @ w_down[e]

    ybuf = lax.map(expert_block, (buf.reshape(NB, MOE_BLOCK, D_MODEL), block_e)).reshape(P, D_MODEL)
    y = jnp.zeros((T, D_MODEL), h.dtype).at[tok_s].add(ybuf[slot] * gate_s[:, None].astype(h.dtype))
    return y.reshape(shp)


def hybrid_layer(x, pos0, ret_s0, k_past, v_past, n_past_valid, keep_rows,
                 norm_mix_w, w_in, ret_gn_w, w_ret_o, q_norm_w, k_norm_w, rel_bias,
                 w_att_o, b_gate, w_out, norm_ffn_w, w_router_g, b_router_g,
                 w_router_e, b_router_e, w_up, w_down):
    B, S, _ = x.shape
    h = rms_norm(x, norm_mix_w)
    proj = h @ w_in
    rq, rk, rv, rg, aq, ak, av, g_ret, g_att = jnp.split(proj, in_split_points(), axis=-1)

    pos = pos0 + jnp.arange(S)
    rq = rotary(rq.reshape(B, S, N_RET_HEADS, RET_DK), pos)
    rk = rotary(rk.reshape(B, S, N_RET_HEADS, RET_DK), pos) * (RET_DK ** -0.5)
    rv = rv.reshape(B, S, N_RET_HEADS, RET_DV)
    o_ret, s_new = retention(rq, rk, rv, ret_s0)
    o_ret = head_group_norm(o_ret, ret_gn_w).reshape(B, S, RET_V).astype(x.dtype)
    y_ret = (jax.nn.silu(rg) * o_ret) @ w_ret_o

    aq = head_rms_norm(aq.reshape(B, S, N_ATT_HEADS, ATT_HD), q_norm_w)
    ak = head_rms_norm(ak.reshape(B, S, N_ATT_HEADS, ATT_HD), k_norm_w)
    av = av.reshape(B, S, N_ATT_HEADS, ATT_HD)
    y_att = chunk_band_attention(aq, ak, av, k_past, v_past, n_past_valid, rel_bias) @ w_att_o

    merged = jax.nn.sigmoid(g_ret + b_gate[0]) * y_ret + jax.nn.sigmoid(g_att + b_gate[1]) * y_att
    x = x + merged @ w_out

    x = x + hier_moe(rms_norm(x, norm_ffn_w), w_router_g, b_router_g, w_router_e, b_router_e, w_up, w_down)
    return x, s_new, ak[:, S - keep_rows:], av[:, S - keep_rows:]


def setup_inputs(seed: int = 0) -> dict:
    key = jax.random.key(seed)
    ks = jax.random.split(key, 24)
    f32 = jnp.float32
    att_cache_len = min(ATT_BAND, PAST_LEN)

    def nrm(k, shape, scale):
        return jax.random.normal(k, shape, f32) * scale

    return {
        'x_prompt': nrm(ks[0], (BATCH, SEQ, D_MODEL), 1.0),
        'x_sample': nrm(ks[1], (DEC_BATCH, DEC_SEQ, D_MODEL), 1.0),
        'state_ret': nrm(ks[2], (DEPTH, DEC_BATCH, N_RET_HEADS, RET_DK, RET_DV), 0.3),
        'cache_att_k': nrm(ks[3], (DEPTH, DEC_BATCH, att_cache_len, N_ATT_HEADS, ATT_HD), 1.0),
        'cache_att_v': nrm(ks[4], (DEPTH, DEC_BATCH, att_cache_len, N_ATT_HEADS, ATT_HD), 1.0),
        'norm_mix_w': 1.0 + nrm(ks[5], (DEPTH, D_MODEL), 0.02),
        'w_in': nrm(ks[6], (DEPTH, D_MODEL, IN_COLS), D_MODEL ** -0.5),
        'ret_gn_w': 1.0 + nrm(ks[7], (DEPTH, N_RET_HEADS, RET_DV), 0.02),
        'w_ret_o': nrm(ks[8], (DEPTH, RET_V, D_MODEL), RET_V ** -0.5),
        'q_norm_w': 1.0 + nrm(ks[9], (DEPTH, ATT_HD), 0.02),
        'k_norm_w': 1.0 + nrm(ks[10], (DEPTH, ATT_HD), 0.02),
        'rel_bias': nrm(ks[11], (DEPTH, N_ATT_HEADS, 2 * MAX_REL + 1), 0.5),
        'w_att_o': nrm(ks[12], (DEPTH, ATT_W, D_MODEL), ATT_W ** -0.5),
        'b_gate': nrm(ks[13], (DEPTH, 2, D_MODEL), 0.1),
        'w_out': nrm(ks[14], (DEPTH, D_MODEL, D_MODEL), D_MODEL ** -0.5),
        'norm_ffn_w': 1.0 + nrm(ks[15], (DEPTH, D_MODEL), 0.02),
        'w_router_g': nrm(ks[16], (DEPTH, D_MODEL, N_GROUPS), D_MODEL ** -0.5),
        'b_router_g': nrm(ks[17], (DEPTH, N_GROUPS), 0.01),
        'w_router_e': nrm(ks[18], (DEPTH, D_MODEL, N_EXPERTS), D_MODEL ** -0.5),
        'b_router_e': nrm(ks[19], (DEPTH, N_EXPERTS), 0.01),
        'w_up': nrm(ks[20], (DEPTH, N_EXPERTS, D_MODEL, 2 * EXPERT_FF), D_MODEL ** -0.5),
        'w_down': nrm(ks[21], (DEPTH, N_EXPERTS, EXPERT_FF, D_MODEL), EXPERT_FF ** -0.5),
    }


def reference(x_prompt, x_sample, state_ret, cache_att_k, cache_att_v, norm_mix_w, w_in, ret_gn_w,
              w_ret_o, q_norm_w, k_norm_w, rel_bias, w_att_o, b_gate, w_out, norm_ffn_w,
              w_router_g, b_router_g, w_router_e, b_router_e, w_up, w_down):
    b_p, s_p, _ = x_prompt.shape
    s_s = x_sample.shape[1]
    past_rows = cache_att_k.shape[2]
    keep_prompt = min(ATT_BAND, s_p)
    zero_state = jnp.zeros((b_p, N_RET_HEADS, RET_DK, RET_DV), jnp.float32)
    zero_kv = jnp.zeros((b_p, ATT_BAND, N_ATT_HEADS, ATT_HD), x_prompt.dtype)
    y_p, y_s = x_prompt, x_sample
    rs_p, rs_s, k_p, v_p, k_s, v_s = [], [], [], [], [], []
    for l in range(DEPTH):
        lw = (norm_mix_w[l], w_in[l], ret_gn_w[l], w_ret_o[l], q_norm_w[l], k_norm_w[l], rel_bias[l],
              w_att_o[l], b_gate[l], w_out[l], norm_ffn_w[l], w_router_g[l], b_router_g[l],
              w_router_e[l], b_router_e[l], w_up[l], w_down[l])
        y_p, st, k_new, v_new = hybrid_layer(y_p, 0, zero_state, zero_kv, zero_kv, 0, keep_prompt, *lw)
        rs_p.append(st.astype(x_prompt.dtype))
        k_p.append(k_new)
        v_p.append(v_new)
        y_s, st, k_new, v_new = hybrid_layer(y_s, PAST_LEN, state_ret[l], cache_att_k[l], cache_att_v[l],
                                             past_rows, s_s, *lw)
        rs_s.append(st.astype(state_ret.dtype))
        k_s.append(k_new)
        v_s.append(v_new)
    return (y_p, y_s, jnp.stack(rs_p), jnp.stack(rs_s), jnp.stack(k_p), jnp.stack(v_p), jnp.stack(k_s), jnp.stack(v_s))
```

```python
import functools

import jax
import jax.numpy as jnp
from jax import lax
from jax.experimental import pallas as pl
from jax.experimental.pallas import tpu as pltpu

F32 = jnp.float32
BF16 = jnp.bfloat16

EPS = 1e-6
NEG_INF = -1e30
ROPE_BASE = 10000.0
PAST_LEN = 1024
CHUNK = 64
TOP_K = 2

V7X_VMEM_LIMIT_BYTES = 56 * 1024 * 1024
LANES = 128
NT_DIMS = (((1,), (1,)), ((), ()))


def _params(*sem):
    return pltpu.CompilerParams(dimension_semantics=sem, vmem_limit_bytes=V7X_VMEM_LIMIT_BYTES)


def _tile(n, pref):
    t = min(n, pref)
    while n % t:
        t //= 2
    assert t >= 8, (n, pref)
    return t


def _rmsnorm_kernel(x_ref, w_ref, o_ref):
    x = x_ref[...]
    ms = jnp.mean(x * x, axis=-1, keepdims=True)
    o_ref[...] = ((x * lax.rsqrt(ms + EPS)) * w_ref[...]).astype(o_ref.dtype)


def _rmsnorm(x, w):
    T, D = x.shape
    tm = _tile(T, 512)
    return pl.pallas_call(
        _rmsnorm_kernel,
        out_shape=jax.ShapeDtypeStruct((T, D), BF16),
        grid=(T // tm,),
        in_specs=[pl.BlockSpec((tm, D), lambda i: (i, 0)), pl.BlockSpec((1, D), lambda i: (0, 0))],
        out_specs=pl.BlockSpec((tm, D), lambda i: (i, 0)),
        compiler_params=_params("parallel"),
        name="rmsnorm",
    )(x, w.reshape(1, D))


def _inproj_kernel(bounds, dk, hd, k_scale, h_ref, w_ref, cos_ref, sin_ref, ep_ref, o_ref):
    b0, b1, b2, b4, b5 = bounds
    j = pl.program_id(1)
    acc = jnp.dot(h_ref[...], w_ref[...], preferred_element_type=F32)
    tn = acc.shape[1]
    half = dk // 2

    def rotary(scale):
        cos = cos_ref[...]
        sin = sin_ref[...]
        for hh in range(tn // dk):
            t1 = acc[:, hh * dk : hh * dk + half]
            t2 = acc[:, hh * dk + half : (hh + 1) * dk]
            o1 = t1 * cos - t2 * sin
            o2 = t1 * sin + t2 * cos
            if scale is not None:
                o1 = o1 * scale
                o2 = o2 * scale
            o_ref[:, hh * dk : hh * dk + half] = o1.astype(o_ref.dtype)
            o_ref[:, hh * dk + half : (hh + 1) * dk] = o2.astype(o_ref.dtype)

    @pl.when(j < b0)
    def _():
        rotary(None)

    @pl.when((j >= b0) & (j < b1))
    def _():
        rotary(k_scale)

    @pl.when(((j >= b1) & (j < b2)) | ((j >= b4) & (j < b5)))
    def _():
        o_ref[...] = acc.astype(o_ref.dtype)

    @pl.when((j >= b2) & (j < b4))
    def _():
        for hh in range(tn // hd):
            t = acc[:, hh * hd : (hh + 1) * hd]
            ms = jnp.mean(t * t, axis=-1, keepdims=True)
            y = (t * lax.rsqrt(ms + EPS)) * ep_ref[:, hh * hd : (hh + 1) * hd]
            o_ref[:, hh * hd : (hh + 1) * hd] = y.astype(o_ref.dtype)

    @pl.when(j >= b5)
    def _():
        o_ref[...] = jax.nn.sigmoid(acc + ep_ref[...]).astype(o_ref.dtype)


def _inproj(h, w, cos, sin, ep, dims):
    T, D = h.shape
    N = w.shape[1]
    ret_qk, ret_v, att_w, dk, hd = dims
    tm = _tile(T, 1024)
    tn = 1024
    for s in (ret_qk, ret_v, att_w, D):
        tn = min(tn, s)
    for s in (ret_qk, ret_v, att_w, D):
        assert s % tn == 0
    assert tn % dk == 0 and tn % hd == 0
    b0 = ret_qk // tn
    b1 = 2 * b0
    b2 = b1 + 2 * ret_v // tn
    b4 = b2 + 2 * att_w // tn
    b5 = b4 + att_w // tn
    body = functools.partial(_inproj_kernel, (b0, b1, b2, b4, b5), dk, hd, float(dk) ** -0.5)
    return pl.pallas_call(
        body,
        out_shape=jax.ShapeDtypeStruct((T, N), BF16),
        grid=(T // tm, N // tn),
        in_specs=[
            pl.BlockSpec((tm, D), lambda i, j: (i, 0)),
            pl.BlockSpec((D, tn), lambda i, j: (0, j)),
            pl.BlockSpec((tm, dk // 2), lambda i, j: (i, 0)),
            pl.BlockSpec((tm, dk // 2), lambda i, j: (i, 0)),
            pl.BlockSpec((1, tn), lambda i, j: (0, j)),
        ],
        out_specs=pl.BlockSpec((tm, tn), lambda i, j: (i, j)),
        compiler_params=_params("parallel", "arbitrary"),
        name="inproj",
    )(h, w, cos, sin, ep)


def _retention_kernel(C, G, q_ref, k_ref, v_ref, g_ref, s0_ref, intra_ref, cross_ref, sdec_ref,
                      cdec_ref, gnw_ref, *rest):
    o_ref, s_ref = rest[-2], rest[-1]
    n = pl.program_id(2)

    @pl.when(n == 0)
    def _():
        s_ref[0, 0] = s0_ref[0, 0]

    def body(g, carry):
        r0 = pl.multiple_of(g * C, C)
        q = q_ref[pl.ds(r0, C), :]
        k = k_ref[pl.ds(r0, C), :]
        v = v_ref[pl.ds(r0, C), :]
        s = s_ref[0, 0]
        scores = lax.dot_general(q, k, NT_DIMS, preferred_element_type=F32) * intra_ref[0]
        o = jnp.dot(scores.astype(BF16), v, preferred_element_type=F32)
        o = o + jnp.dot(q, s.astype(BF16), preferred_element_type=F32) * cross_ref[0]
        kd_t = (k.astype(F32) * sdec_ref[0]).T.astype(BF16)
        s_ref[0, 0] = cdec_ref[0] * s + jnp.dot(kd_t, v, preferred_element_type=F32)
        mu = jnp.mean(o, axis=-1, keepdims=True)
        d = o - mu
        var = jnp.mean(d * d, axis=-1, keepdims=True)
        y = d * lax.rsqrt(var + EPS) * gnw_ref[0]
        gate = g_ref[pl.ds(r0, C), :].astype(F32)
        o_ref[pl.ds(r0, C), :] = (gate * jax.nn.sigmoid(gate) * y).astype(o_ref.dtype)
        return carry

    lax.fori_loop(0, G, body, 0)


def _decay_tables(H, C):
    log_gamma = jnp.log1p(-jnp.exp2(-5.0 - jnp.arange(H, dtype=F32)))
    idx = jnp.arange(C, dtype=F32)
    diff = idx[:, None] - idx[None, :]
    intra = jnp.where(diff >= 0, jnp.exp(log_gamma[:, None, None] * jnp.maximum(diff, 0.0)), 0.0)
    cross = jnp.exp(log_gamma[:, None] * (idx + 1.0))[:, :, None]
    sdec = jnp.exp(log_gamma[:, None] * (C - 1.0 - idx))[:, :, None]
    cdec = jnp.exp(log_gamma * C)[:, None, None]
    return intra, cross, sdec, cdec


def _retention(proj, s0, gnw, out_prev, row0, B, S, C, G, dims):
    ret_qk, ret_v, att_w, dk, hd = dims
    T = proj.shape[0]
    H = ret_qk // dk
    dv = ret_v // H
    rows = C * G
    nblk = S // rows
    assert S % rows == 0 and row0 % rows == 0
    blk0 = row0 // rows
    intra, cross, sdec, cdec = _decay_tables(H, C)
    cdec = jnp.broadcast_to(cdec, (H, 1, dv))
    kcol = ret_qk // dk
    vcol = 2 * ret_qk // dv
    gcol = (2 * ret_qk + ret_v) // dv
    assert (2 * ret_qk) % dv == 0 and (2 * ret_qk + ret_v) % dv == 0

    def rowmap(col0):
        return lambda b, h, n: (blk0 + b * nblk + n, col0 + h)

    in_specs = [
        pl.BlockSpec((rows, dk), rowmap(0)),
        pl.BlockSpec((rows, dk), rowmap(kcol)),
        pl.BlockSpec((rows, dv), rowmap(vcol)),
        pl.BlockSpec((rows, dv), rowmap(gcol)),
        pl.BlockSpec((1, 1, dk, dv), lambda b, h, n: (b, h, 0, 0)),
        pl.BlockSpec((1, C, C), lambda b, h, n: (h, 0, 0)),
        pl.BlockSpec((1, C, 1), lambda b, h, n: (h, 0, 0)),
        pl.BlockSpec((1, C, 1), lambda b, h, n: (h, 0, 0)),
        pl.BlockSpec((1, 1, dv), lambda b, h, n: (h, 0, 0)),
        pl.BlockSpec((1, 1, dv), lambda b, h, n: (h, 0, 0)),
    ]
    args = [proj, proj, proj, proj, s0, intra, cross, sdec, cdec, gnw.reshape(H, 1, dv)]
    aliases = {}
    if out_prev is not None:
        in_specs.append(pl.BlockSpec(memory_space=pl.ANY))
        args.append(out_prev)
        aliases = {len(args) - 1: 0}
    return pl.pallas_call(
        functools.partial(_retention_kernel, C, G),
        out_shape=(jax.ShapeDtypeStruct((T, ret_v), BF16), jax.ShapeDtypeStruct((B, H, dk, dv), F32)),
        grid=(B, H, nblk),
        in_specs=in_specs,
        out_specs=(
            pl.BlockSpec((rows, dv), rowmap(0)),
            pl.BlockSpec((1, 1, dk, dv), lambda b, h, n: (b, h, 0, 0)),
        ),
        input_output_aliases=aliases,
        compiler_params=_params("parallel", "parallel", "arbitrary"),
        name="retention",
    )(*args)


def _attention_kernel(C, R, NC, H, hd, mask_first, q_ref, kp_ref, kc_ref, vp_ref, vc_ref, bias_ref,
                      *rest):
    o_ref, kwin, vwin = rest[-3], rest[-2], rest[-1]
    i = pl.program_id(1)
    rows = NC * C
    kwin[0:R, :] = kp_ref[...].reshape(R, H * hd).astype(BF16)
    kwin[R : R + rows, :] = kc_ref[...]
    vwin[0:R, :] = vp_ref[...].reshape(R, H * hd).astype(BF16)
    vwin[R : R + rows, :] = vc_ref[...]
    scale = float(hd) ** -0.5

    def chunk(c, carry):
        r0 = pl.multiple_of(c * C, C)
        if mask_first:
            col = lax.broadcasted_iota(jnp.int32, (C, R + C), 1)
            valid = (col + (i * rows - R + c * C)) >= 0
        for h in range(H):
            lanes = slice(h * hd, (h + 1) * hd)
            q = q_ref[pl.ds(r0, C), lanes]
            kb = kwin[pl.ds(r0, R + C), lanes]
            vb = vwin[pl.ds(r0, R + C), lanes]
            s = lax.dot_general(q, kb, NT_DIMS, preferred_element_type=F32) * scale + bias_ref[h]
            if mask_first:
                s = jnp.where(valid, s, NEG_INF)
            m = jnp.max(s, axis=-1, keepdims=True)
            p = jnp.exp(s - m)
            l = jnp.sum(p, axis=-1, keepdims=True)
            o = jnp.dot(p.astype(BF16), vb, preferred_element_type=F32) * (1.0 / l)
            o_ref[pl.ds(r0, C), lanes] = o.astype(o_ref.dtype)
        return carry

    lax.fori_loop(0, NC, chunk, 0)


def _attention(proj, k_past, v_past, bias, out_prev, row0, B, S, dims):
    ret_qk, ret_v, att_w, dk, hd = dims
    T = proj.shape[0]
    H = att_w // hd
    C = min(CHUNK, S)
    R = bias.shape[2] - C
    fresh = k_past is None
    rows = R if fresh else S
    assert S % rows == 0 and row0 % rows == 0 and rows % C == 0
    nblk = S // rows
    blk0 = row0 // rows
    qcol = (2 * ret_qk + 2 * ret_v) // att_w
    assert (2 * ret_qk + 2 * ret_v) % att_w == 0

    def cur(col):
        return pl.BlockSpec((rows, att_w), lambda b, i: (blk0 + b * nblk + i, col))

    if fresh:
        def prev(col):
            return pl.BlockSpec((rows, att_w), lambda b, i: (blk0 + b * nblk + jnp.maximum(i - 1, 0), col))

        kp_spec, vp_spec = prev(qcol + 1), prev(qcol + 2)
        kp_arg, vp_arg = proj, proj
    else:
        kp_spec = pl.BlockSpec((1, R, att_w), lambda b, i: (b, 0, 0))
        vp_spec = kp_spec
        kp_arg, vp_arg = k_past, v_past
    in_specs = [cur(qcol), kp_spec, cur(qcol + 1), vp_spec, cur(qcol + 2),
                pl.BlockSpec((H, C, R + C), lambda b, i: (0, 0, 0))]
    args = [proj, kp_arg, proj, vp_arg, proj, bias]
    aliases = {}
    if out_prev is not None:
        in_specs.append(pl.BlockSpec(memory_space=pl.ANY))
        args.append(out_prev)
        aliases = {len(args) - 1: 0}
    return pl.pallas_call(
        functools.partial(_attention_kernel, C, R, rows // C, H, hd, fresh),
        out_shape=jax.ShapeDtypeStruct((T, att_w), BF16),
        grid=(B, nblk),
        in_specs=in_specs,
        out_specs=pl.BlockSpec((rows, att_w), lambda b, i: (blk0 + b * nblk + i, 0)),
        scratch_shapes=[pltpu.VMEM((R + rows, att_w), BF16), pltpu.VMEM((R + rows, att_w), BF16)],
        input_output_aliases=aliases,
        compiler_params=_params("parallel", "arbitrary"),
        name="attention",
    )(*args)


def _merge_kernel(ret_ref, wr_ref, att_ref, wa_ref, gr_ref, ga_ref, o_ref):
    y_ret = jnp.dot(ret_ref[...], wr_ref[...], preferred_element_type=F32)
    y_att = jnp.dot(att_ref[...], wa_ref[...], preferred_element_type=F32)
    merged = gr_ref[...].astype(F32) * y_ret + ga_ref[...].astype(F32) * y_att
    o_ref[...] = merged.astype(o_ref.dtype)


def _merge(gated, att, proj, w_ret_o, w_att_o):
    T, ret_v = gated.shape
    att_w = att.shape[1]
    D = w_ret_o.shape[1]
    in_cols = proj.shape[1]
    tm = _tile(T, 256)
    tn = _tile(D, 1024)
    gcol = (in_cols - 2 * D) // tn
    assert (in_cols - 2 * D) % tn == 0
    return pl.pallas_call(
        _merge_kernel,
        out_shape=jax.ShapeDtypeStruct((T, D), BF16),
        grid=(D // tn, T // tm),
        in_specs=[
            pl.BlockSpec((tm, ret_v), lambda j, i: (i, 0)),
            pl.BlockSpec((ret_v, tn), lambda j, i: (0, j)),
            pl.BlockSpec((tm, att_w), lambda j, i: (i, 0)),
            pl.BlockSpec((att_w, tn), lambda j, i: (0, j)),
            pl.BlockSpec((tm, tn), lambda j, i: (i, gcol + j)),
            pl.BlockSpec((tm, tn), lambda j, i: (i, gcol + D // tn + j)),
        ],
        out_specs=pl.BlockSpec((tm, tn), lambda j, i: (i, j)),
        compiler_params=_params("parallel", "parallel"),
        name="merge",
    )(gated, w_ret_o, att, w_att_o, proj, proj)


def _outproj_kernel(NG, NE, m_ref, w_ref, x_ref, nw_ref, wr_ref, br_ref, x1_ref, h2_ref, ids_ref,
                    gates_ref):
    x1 = x_ref[...] + jnp.dot(m_ref[...], w_ref[...], preferred_element_type=F32)
    x1_ref[...] = x1
    ms = jnp.mean(x1 * x1, axis=-1, keepdims=True)
    h2 = (x1 * lax.rsqrt(ms + EPS)) * nw_ref[...]
    h2_ref[...] = h2
    logits = jnp.dot(h2, wr_ref[...], preferred_element_type=F32, precision=lax.Precision.HIGHEST)
    logits = logits + br_ref[...]
    epg = NE // NG
    lane = lax.broadcasted_iota(jnp.int32, logits.shape, 1).astype(F32)
    big = float(LANES)
    gmask = lane < NG
    gl = jnp.where(gmask, logits, NEG_INF)
    gm = jnp.max(gl, axis=-1, keepdims=True)
    gsel = jnp.min(jnp.where(gl == gm, lane, big), axis=-1, keepdims=True)
    pg = 1.0 / jnp.sum(jnp.where(gmask, jnp.exp(gl - gm), 0.0), axis=-1, keepdims=True)
    lo = NG + gsel * epg
    ingrp = (lane >= lo) & (lane < lo + epg)
    el = jnp.where(ingrp, logits, NEG_INF)
    v1 = jnp.max(el, axis=-1, keepdims=True)
    i1 = jnp.min(jnp.where(el == v1, lane, big), axis=-1, keepdims=True)
    el2 = jnp.where(lane == i1, NEG_INF, el)
    v2 = jnp.max(el2, axis=-1, keepdims=True)
    i2 = jnp.min(jnp.where(el2 == v2, lane, big), axis=-1, keepdims=True)
    t = jnp.exp(v2 - v1)
    den = 1.0 / (1.0 + t)
    g1 = pg * den
    g2 = pg * (t * den)
    ids = jnp.where(lane == 0.0, i1 - NG, jnp.where(lane == 1.0, i2 - NG, 0.0))
    ids_ref[...] = ids.astype(jnp.int32)
    gates_ref[...] = jnp.where(lane == 0.0, g1, jnp.where(lane == 1.0, g2, 0.0))


def _outproj(merged, w_out, x, nw, wr, br, NG, NE):
    T, D = x.shape
    tm = _tile(T, 256)
    row = lambda i: (i, 0)
    const = lambda i: (0, 0)
    return pl.pallas_call(
        functools.partial(_outproj_kernel, NG, NE),
        out_shape=(
            jax.ShapeDtypeStruct((T, D), F32),
            jax.ShapeDtypeStruct((T, D), F32),
            jax.ShapeDtypeStruct((T, LANES), jnp.int32),
            jax.ShapeDtypeStruct((T, LANES), F32),
        ),
        grid=(T // tm,),
        in_specs=[
            pl.BlockSpec((tm, D), row),
            pl.BlockSpec((D, D), const),
            pl.BlockSpec((tm, D), row),
            pl.BlockSpec((1, D), const),
            pl.BlockSpec((D, LANES), const),
            pl.BlockSpec((1, LANES), const),
        ],
        out_specs=(
            pl.BlockSpec((tm, D), row),
            pl.BlockSpec((tm, D), row),
            pl.BlockSpec((tm, LANES), row),
            pl.BlockSpec((tm, LANES), row),
        ),
        compiler_params=_params("parallel"),
        name="outproj_router",
    )(merged, w_out, x, nw.reshape(1, D), wr, br)


def _slot_tables(ids, NE, blk, NB):
    e_flat = ids[:, :TOP_K].reshape(-1)
    onehot = (e_flat[:, None] == jnp.arange(NE, dtype=jnp.int32)[None, :]).astype(jnp.int32)
    csum = jnp.cumsum(onehot, axis=0)
    counts = csum[-1]
    rank = jnp.sum(onehot * csum, axis=1) - 1
    pcounts = (counts + blk - 1) // blk * blk
    pends = jnp.cumsum(pcounts)
    pstarts = pends - pcounts
    slots = jnp.sum(onehot * pstarts[None, :], axis=1) + rank
    blk_start = jnp.arange(NB, dtype=jnp.int32) * blk
    block_e = jnp.minimum(jnp.sum((pends[None, :] <= blk_start[:, None]).astype(jnp.int32), axis=1), NE - 1)
    n_used = (pends[-1] // blk).reshape(1)
    return slots.astype(jnp.int32), block_e.astype(jnp.int32), n_used.astype(jnp.int32)


def _row_copy(src_ref, src_row, dst_ref, dst_row, sem):
    return pltpu.make_async_copy(src_ref.at[pl.ds(src_row, 1)], dst_ref.at[pl.ds(dst_row, 1)], sem)


def _dispatch_kernel(tm, slots_ref, h_ref, init_ref, buf_ref, sem):
    del init_ref
    base = pl.program_id(0) * (TOP_K * tm)

    def start(r, carry):
        for kk in range(TOP_K):
            _row_copy(h_ref, r, buf_ref, slots_ref[base + TOP_K * r + kk], sem).start()
        return carry

    lax.fori_loop(0, tm, start, 0)

    def wait(r, carry):
        for kk in range(TOP_K):
            _row_copy(h_ref, 0, buf_ref, 0, sem).wait()
        return carry

    lax.fori_loop(0, tm, wait, 0)


def _dispatch(h2, slots, P):
    T, D = h2.shape
    tm = _tile(T, 512)
    init = jnp.zeros((P, D), F32)
    grid_spec = pltpu.PrefetchScalarGridSpec(
        num_scalar_prefetch=1,
        grid=(T // tm,),
        in_specs=[
            pl.BlockSpec((tm, D), lambda i, s: (i, 0)),
            pl.BlockSpec(memory_space=pl.ANY),
        ],
        out_specs=pl.BlockSpec(memory_space=pl.ANY),
        scratch_shapes=[pltpu.SemaphoreType.DMA(())],
    )
    return pl.pallas_call(
        functools.partial(_dispatch_kernel, tm),
        out_shape=jax.ShapeDtypeStruct((P, D), F32),
        grid_spec=grid_spec,
        input_output_aliases={2: 0},
        compiler_params=_params("arbitrary"),
        name="moe_dispatch",
    )(slots, h2, init)


def _experts_kernel(block_e_ref, n_used_ref, x_ref, wu_ref, wd_ref, o_ref):
    i = pl.program_id(0)

    @pl.when(i < n_used_ref[0])
    def _():
        x = x_ref[...].astype(BF16)
        up = jnp.dot(x, wu_ref[0], preferred_element_type=F32)
        ff = up.shape[1] // 2
        a = up[:, :ff]
        b = up[:, ff:]
        hmid = (a * jax.nn.sigmoid(a) * b).astype(BF16)
        o_ref[...] = jnp.dot(hmid, wd_ref[0], preferred_element_type=F32)

    @pl.when(i >= n_used_ref[0])
    def _():
        o_ref[...] = jnp.zeros_like(o_ref)


def _experts(buf, block_e, n_used, w_up, w_down, blk):
    P, D = buf.shape
    NB = P // blk
    ff2 = w_up.shape[2]
    grid_spec = pltpu.PrefetchScalarGridSpec(
        num_scalar_prefetch=2,
        grid=(NB,),
        in_specs=[
            pl.BlockSpec((blk, D), lambda i, be, nu: (i, 0)),
            pl.BlockSpec((1, D, ff2), lambda i, be, nu: (be[i], 0, 0)),
            pl.BlockSpec((1, ff2 // 2, D), lambda i, be, nu: (be[i], 0, 0)),
        ],
        out_specs=pl.BlockSpec((blk, D), lambda i, be, nu: (i, 0)),
    )
    return pl.pallas_call(
        _experts_kernel,
        out_shape=jax.ShapeDtypeStruct((P, D), F32),
        grid_spec=grid_spec,
        compiler_params=_params("arbitrary"),
        name="moe_experts",
    )(block_e, n_used, buf, w_up, w_down)


def _combine_kernel(tm, slots_ref, x_ref, gates_ref, y_ref, o_ref, rows, sem):
    i = pl.program_id(0)
    n = pl.num_programs(0)

    def fetch(tile, b):
        base = tile * (TOP_K * tm)

        def start(r, carry):
            for kk in range(TOP_K):
                _row_copy(y_ref, slots_ref[base + TOP_K * r + kk], rows.at[b, kk], r, sem.at[b]).start()
            return carry

        lax.fori_loop(0, tm, start, 0)

    @pl.when(i == 0)
    def _():
        fetch(0, 0)

    @pl.when(i + 1 < n)
    def _():
        fetch(i + 1, (i + 1) % 2)

    b = i % 2

    def wait(r, carry):
        for kk in range(TOP_K):
            _row_copy(y_ref, 0, rows.at[b, kk], 0, sem.at[b]).wait()
        return carry

    lax.fori_loop(0, tm, wait, 0)
    g = gates_ref[...]
    y = g[:, 0:1] * rows[b, 0]
    for kk in range(1, TOP_K):
        y = y + g[:, kk : kk + 1] * rows[b, kk]
    o_ref[...] = x_ref[...] + y


def _combine(x1, ybuf, slots, gates):
    T, D = x1.shape
    tm = _tile(T, 256)
    grid_spec = pltpu.PrefetchScalarGridSpec(
        num_scalar_prefetch=1,
        grid=(T // tm,),
        in_specs=[
            pl.BlockSpec((tm, D), lambda i, s: (i, 0)),
            pl.BlockSpec((tm, LANES), lambda i, s: (i, 0)),
            pl.BlockSpec(memory_space=pl.ANY),
        ],
        out_specs=pl.BlockSpec((tm, D), lambda i, s: (i, 0)),
        scratch_shapes=[pltpu.VMEM((2, TOP_K, tm, D), F32), pltpu.SemaphoreType.DMA((2,))],
    )
    return pl.pallas_call(
        functools.partial(_combine_kernel, tm),
        out_shape=jax.ShapeDtypeStruct((T, D), F32),
        grid_spec=grid_spec,
        compiler_params=_params("arbitrary"),
        name="moe_combine",
    )(slots, x1, gates, ybuf)


def _rope_tables(Bp, Sp, Bs, Ss, half):
    freqs = ROPE_BASE ** (-jnp.arange(half, dtype=F32) / half)
    pos = jnp.concatenate([jnp.tile(jnp.arange(Sp), Bp), jnp.tile(PAST_LEN + jnp.arange(Ss), Bs)])
    ang = pos.astype(F32)[:, None] * freqs[None, :]
    return jnp.cos(ang), jnp.sin(ang)


def _band_bias(rel_table, C, R):
    max_rel = (rel_table.shape[1] - 1) // 2
    a = jnp.arange(C)
    b = jnp.arange(R + C)
    rel = a[:, None] - (b[None, :] - R)
    return rel_table[:, jnp.clip(rel, -max_rel, max_rel) + max_rel].astype(F32)


def kernel(x_prompt, x_sample, state_ret, cache_att_k, cache_att_v, norm_mix_w, w_in, ret_gn_w,
           w_ret_o, q_norm_w, k_norm_w, rel_bias, w_att_o, b_gate, w_out, norm_ffn_w, w_router_g,
           b_router_g, w_router_e, b_router_e, w_up, w_down):
    Bp, Sp, D = x_prompt.shape
    Bs, Ss, _ = x_sample.shape
    depth = w_in.shape[0]
    _, _, H_ret, dk, dv = state_ret.shape
    _, _, R, H_att, hd = cache_att_k.shape
    ret_qk, ret_v, att_w = H_ret * dk, H_ret * dv, H_att * hd
    dims = (ret_qk, ret_v, att_w, dk, hd)
    NG, NE = w_router_g.shape[2], w_router_e.shape[2]
    Tp, Ts = Bp * Sp, Bs * Ss
    T = Tp + Ts
    keep = min(R, Sp)
    assert NG + NE <= LANES and R % CHUNK == 0

    blk = 256
    NB = (T * TOP_K + NE * (blk - 1) + blk - 1) // blk
    C_p = 256 if Sp % 256 == 0 else min(CHUNK, Sp)
    G_p = max(1, min(512, Sp) // C_p)
    C_s = min(CHUNK, Ss)

    x = jnp.concatenate([x_prompt.reshape(Tp, D), x_sample.reshape(Ts, D)], axis=0)
    cos, sin = _rope_tables(Bp, Sp, Bs, Ss, dk // 2)
    zero_state = jnp.zeros((Bp, H_ret, dk, dv), F32)
    kcol = 2 * ret_qk + 2 * ret_v + att_w
    vcol = kcol + att_w

    rs_p, rs_s, k_p, v_p, k_s, v_s = [], [], [], [], [], []
    for l in range(depth):
        ep = jnp.zeros((1, w_in.shape[2]), F32)
        ep = ep.at[0, kcol - att_w : kcol].set(jnp.tile(q_norm_w[l], H_att))
        ep = ep.at[0, kcol:vcol].set(jnp.tile(k_norm_w[l], H_att))
        ep = ep.at[0, vcol + att_w :].set(b_gate[l].reshape(-1))
        wr = jnp.zeros((D, LANES), F32).at[:, :NG].set(w_router_g[l]).at[:, NG : NG + NE].set(w_router_e[l])
        br = jnp.zeros((1, LANES), F32).at[0, :NG].set(b_router_g[l]).at[0, NG : NG + NE].set(b_router_e[l])
        bias = _band_bias(rel_bias[l], min(CHUNK, Sp), R)

        h = _rmsnorm(x, norm_mix_w[l])
        proj = _inproj(h, w_in[l].astype(BF16), cos, sin, ep, dims)

        gated, st_p = _retention(proj, zero_state, ret_gn_w[l], None, 0, Bp, Sp, C_p, G_p, dims)
        gated, st_s = _retention(proj, state_ret[l], ret_gn_w[l], gated, Tp, Bs, Ss, C_s, 1, dims)
        att = _attention(proj, None, None, bias, None, 0, Bp, Sp, dims)
        att = _attention(proj, cache_att_k[l].reshape(Bs, R, att_w), cache_att_v[l].reshape(Bs, R, att_w),
                         bias, att, Tp, Bs, Ss, dims)

        merged = _merge(gated, att, proj, w_ret_o[l].astype(BF16), w_att_o[l].astype(BF16))
        x1, h2, ids, gates = _outproj(merged, w_out[l].astype(BF16), x, norm_ffn_w[l], wr, br, NG, NE)

        slots, block_e, n_used = _slot_tables(ids, NE, blk, NB)
        buf = _dispatch(h2, slots, NB * blk)
        ybuf = _experts(buf, block_e, n_used, w_up[l].astype(BF16), w_down[l].astype(BF16), blk)
        x = _combine(x1, ybuf, slots, gates)

        rs_p.append(st_p)
        rs_s.append(st_s)
        pp = proj[:Tp].reshape(Bp, Sp, -1)[:, Sp - keep :]
        ps = proj[Tp:].reshape(Bs, Ss, -1)
        k_p.append(pp[..., kcol:vcol].astype(F32).reshape(Bp, keep, H_att, hd))
        v_p.append(pp[..., vcol : vcol + att_w].astype(F32).reshape(Bp, keep, H_att, hd))
        k_s.append(ps[..., kcol:vcol].astype(F32).reshape(Bs, Ss, H_att, hd))
        v_s.append(ps[..., vcol : vcol + att_w].astype(F32).reshape(Bs, Ss, H_att, hd))

    y_p = x[:Tp].reshape(Bp, Sp, D)
    y_s = x[Tp:].reshape(Bs, Ss, D)
    return (y_p, y_s, jnp.stack(rs_p), jnp.stack(rs_s), jnp.stack(k_p), jnp.stack(v_p),
            jnp.stack(k_s), jnp.stack(v_s))
```

```python
import functools

import jax
import jax.numpy as jnp
import numpy as np
from jax import lax
from jax.experimental import pallas as pl
from jax.experimental.pallas import tpu as pltpu

F32 = jnp.float32
BF16 = jnp.bfloat16

EPS = 1e-6
NEG_INF = -1e30
ROPE_BASE = 10000.0
PAST_LEN = 1024
CHUNK = 64
TOP_K = 2

V7X_VMEM_LIMIT_BYTES = 56 * 1024 * 1024
LANES = 128
NT_DIMS = (((1,), (1,)), ((), ()))


def _params(*sem):
    return pltpu.CompilerParams(dimension_semantics=sem, vmem_limit_bytes=V7X_VMEM_LIMIT_BYTES)


def _tile(n, pref):
    t = min(n, pref)
    while n % t:
        t //= 2
    assert t >= 8, (n, pref)
    return t


def _rmsnorm_kernel(x_ref, w_ref, o_ref):
    x = x_ref[...]
    ms = jnp.mean(x * x, axis=-1, keepdims=True)
    o_ref[...] = ((x * lax.rsqrt(ms + EPS)) * w_ref[...]).astype(o_ref.dtype)


def _rmsnorm(x, w):
    T, D = x.shape
    tm = _tile(T, 512)
    return pl.pallas_call(
        _rmsnorm_kernel,
        out_shape=jax.ShapeDtypeStruct((T, D), BF16),
        grid=(T // tm,),
        in_specs=[pl.BlockSpec((tm, D), lambda i: (i, 0)), pl.BlockSpec((1, D), lambda i: (0, 0))],
        out_specs=pl.BlockSpec((tm, D), lambda i: (i, 0)),
        compiler_params=_params("parallel"),
        name="rmsnorm",
    )(x, w.reshape(1, D))


def _inproj_kernel(bounds, dk, hd, k_scale, h_ref, w_ref, cos_ref, sin_ref, ep_ref, o_ref):
    b0, b1, b2, b4, b5 = bounds
    j = pl.program_id(1)
    acc = jnp.dot(h_ref[...], w_ref[...], preferred_element_type=F32)
    tn = acc.shape[1]
    half = dk // 2

    def rotary(scale):
        cos = cos_ref[...]
        sin = sin_ref[...]
        for hh in range(tn // dk):
            t1 = acc[:, hh * dk : hh * dk + half]
            t2 = acc[:, hh * dk + half : (hh + 1) * dk]
            o1 = t1 * cos - t2 * sin
            o2 = t1 * sin + t2 * cos
            if scale is not None:
                o1 = o1 * scale
                o2 = o2 * scale
            o_ref[:, hh * dk : hh * dk + half] = o1.astype(o_ref.dtype)
            o_ref[:, hh * dk + half : (hh + 1) * dk] = o2.astype(o_ref.dtype)

    @pl.when(j < b0)
    def _():
        rotary(None)

    @pl.when((j >= b0) & (j < b1))
    def _():
        rotary(k_scale)

    @pl.when(((j >= b1) & (j < b2)) | ((j >= b4) & (j < b5)))
    def _():
        o_ref[...] = acc.astype(o_ref.dtype)

    @pl.when((j >= b2) & (j < b4))
    def _():
        for hh in range(tn // hd):
            t = acc[:, hh * hd : (hh + 1) * hd]
            ms = jnp.mean(t * t, axis=-1, keepdims=True)
            y = (t * lax.rsqrt(ms + EPS)) * ep_ref[:, hh * hd : (hh + 1) * hd]
            o_ref[:, hh * hd : (hh + 1) * hd] = y.astype(o_ref.dtype)

    @pl.when(j >= b5)
    def _():
        o_ref[...] = jax.nn.sigmoid(acc + ep_ref[...]).astype(o_ref.dtype)


def _inproj(h, w, cos, sin, ep, dims):
    T, D = h.shape
    N = w.shape[1]
    ret_qk, ret_v, att_w, dk, hd = dims
    tm = _tile(T, 1024)
    tn = 1024
    for s in (ret_qk, ret_v, att_w, D):
        tn = min(tn, s)
    for s in (ret_qk, ret_v, att_w, D):
        assert s % tn == 0
    assert tn % dk == 0 and tn % hd == 0
    b0 = ret_qk // tn
    b1 = 2 * b0
    b2 = b1 + 2 * ret_v // tn
    b4 = b2 + 2 * att_w // tn
    b5 = b4 + att_w // tn
    body = functools.partial(_inproj_kernel, (b0, b1, b2, b4, b5), dk, hd, float(dk) ** -0.5)
    return pl.pallas_call(
        body,
        out_shape=jax.ShapeDtypeStruct((T, N), BF16),
        grid=(T // tm, N // tn),
        in_specs=[
            pl.BlockSpec((tm, D), lambda i, j: (i, 0)),
            pl.BlockSpec((D, tn), lambda i, j: (0, j)),
            pl.BlockSpec((tm, dk // 2), lambda i, j: (i, 0)),
            pl.BlockSpec((tm, dk // 2), lambda i, j: (i, 0)),
            pl.BlockSpec((1, tn), lambda i, j: (0, j)),
        ],
        out_specs=pl.BlockSpec((tm, tn), lambda i, j: (i, j)),
        compiler_params=_params("parallel", "arbitrary"),
        name="inproj",
    )(h, w, cos, sin, ep)


def _retention_kernel(C, G, q_ref, k_ref, v_ref, g_ref, s0_ref, intra_ref, cross_ref, sdec_ref,
                      cdec_ref, gnw_ref, *rest):
    o_ref, s_ref = rest[-2], rest[-1]
    n = pl.program_id(2)

    @pl.when(n == 0)
    def _():
        s_ref[0, 0] = s0_ref[0, 0]

    def body(g, carry):
        r0 = pl.multiple_of(g * C, C)
        q = q_ref[pl.ds(r0, C), :]
        k = k_ref[pl.ds(r0, C), :]
        v = v_ref[pl.ds(r0, C), :]
        s = s_ref[0, 0]
        scores = lax.dot_general(q, k, NT_DIMS, preferred_element_type=F32) * intra_ref[0]
        o = jnp.dot(scores.astype(BF16), v, preferred_element_type=F32)
        o = o + jnp.dot(q, s.astype(BF16), preferred_element_type=F32) * cross_ref[0]
        kd_t = (k.astype(F32) * sdec_ref[0]).T.astype(BF16)
        s_ref[0, 0] = cdec_ref[0] * s + jnp.dot(kd_t, v, preferred_element_type=F32)
        mu = jnp.mean(o, axis=-1, keepdims=True)
        d = o - mu
        var = jnp.mean(d * d, axis=-1, keepdims=True)
        y = d * lax.rsqrt(var + EPS) * gnw_ref[0]
        gate = g_ref[pl.ds(r0, C), :].astype(F32)
        o_ref[pl.ds(r0, C), :] = (gate * jax.nn.sigmoid(gate) * y).astype(o_ref.dtype)
        return carry

    lax.fori_loop(0, G, body, 0)


def _decay_tables(H, C):
    log_gamma = jnp.log1p(-jnp.exp2(-5.0 - jnp.arange(H, dtype=F32)))
    idx = jnp.arange(C, dtype=F32)
    diff = idx[:, None] - idx[None, :]
    intra = jnp.where(diff >= 0, jnp.exp(log_gamma[:, None, None] * jnp.maximum(diff, 0.0)), 0.0)
    cross = jnp.exp(log_gamma[:, None] * (idx + 1.0))[:, :, None]
    sdec = jnp.exp(log_gamma[:, None] * (C - 1.0 - idx))[:, :, None]
    cdec = jnp.exp(log_gamma * C)[:, None, None]
    return intra, cross, sdec, cdec


def _retention(proj, s0, s0_b0, gnw, out_prev, row0, B, S, C, G, dims):
    ret_qk, ret_v, att_w, dk, hd = dims
    T = proj.shape[0]
    H = ret_qk // dk
    dv = ret_v // H
    rows = C * G
    nblk = S // rows
    assert S % rows == 0 and row0 % rows == 0
    blk0 = row0 // rows
    intra, cross, sdec, cdec = _decay_tables(H, C)
    cdec = jnp.broadcast_to(cdec, (H, 1, dv))
    kcol = ret_qk // dk
    vcol = 2 * ret_qk // dv
    gcol = (2 * ret_qk + ret_v) // dv
    assert (2 * ret_qk) % dv == 0 and (2 * ret_qk + ret_v) % dv == 0

    def rowmap(col0):
        return lambda b, h, n: (blk0 + b * nblk + n, col0 + h)

    in_specs = [
        pl.BlockSpec((rows, dk), rowmap(0)),
        pl.BlockSpec((rows, dk), rowmap(kcol)),
        pl.BlockSpec((rows, dv), rowmap(vcol)),
        pl.BlockSpec((rows, dv), rowmap(gcol)),
        pl.BlockSpec((1, 1, dk, dv), lambda b, h, n: (s0_b0 + b, h, 0, 0)),
        pl.BlockSpec((1, C, C), lambda b, h, n: (h, 0, 0)),
        pl.BlockSpec((1, C, 1), lambda b, h, n: (h, 0, 0)),
        pl.BlockSpec((1, C, 1), lambda b, h, n: (h, 0, 0)),
        pl.BlockSpec((1, 1, dv), lambda b, h, n: (h, 0, 0)),
        pl.BlockSpec((1, 1, dv), lambda b, h, n: (h, 0, 0)),
    ]
    args = [proj, proj, proj, proj, s0, intra, cross, sdec, cdec, gnw.reshape(H, 1, dv)]
    aliases = {}
    if out_prev is not None:
        in_specs.append(pl.BlockSpec(memory_space=pl.ANY))
        args.append(out_prev)
        aliases = {len(args) - 1: 0}
    return pl.pallas_call(
        functools.partial(_retention_kernel, C, G),
        out_shape=(jax.ShapeDtypeStruct((T, ret_v), BF16), jax.ShapeDtypeStruct((B, H, dk, dv), F32)),
        grid=(B, H, nblk),
        in_specs=in_specs,
        out_specs=(
            pl.BlockSpec((rows, dv), rowmap(0)),
            pl.BlockSpec((1, 1, dk, dv), lambda b, h, n: (b, h, 0, 0)),
        ),
        input_output_aliases=aliases,
        compiler_params=_params("parallel", "parallel", "arbitrary"),
        name="retention",
    )(*args)


def _attention_kernel(C, R, NC, H, hd, mask_first, q_ref, kp_ref, kc_ref, vp_ref, vc_ref, bias_ref,
                      *rest):
    o_ref = rest[-1]
    i = pl.program_id(1)
    SB = NC * C
    prev3d = len(kp_ref.shape) == 3
    prev_off = jnp.where(i > 0, 0.0, NEG_INF) if mask_first else None

    for sb in range(q_ref.shape[0] // SB):
        r0 = sb * SB
        n_prev = max(R - r0, 0)
        c0 = max(r0 - R, 0)
        for h in range(H):
            lanes = slice(h * hd, (h + 1) * hd)
            q = q_ref[r0 : r0 + SB, lanes]
            kc = kc_ref[c0 : r0 + SB, lanes]
            vc = vc_ref[c0 : r0 + SB, lanes]
            sc = lax.dot_general(q, kc, NT_DIMS, preferred_element_type=F32) + bias_ref[h, :, n_prev:]
            m = jnp.max(sc, axis=-1, keepdims=True)
            if n_prev:
                if prev3d:
                    kp = kp_ref[0, r0:R, lanes].astype(BF16)
                    vp = vp_ref[0, r0:R, lanes].astype(BF16)
                else:
                    kp = kp_ref[r0:R, lanes]
                    vp = vp_ref[r0:R, lanes]
                sp = lax.dot_general(q, kp, NT_DIMS, preferred_element_type=F32) + bias_ref[h, :, :n_prev]
                if mask_first:
                    sp = sp + prev_off
                m = jnp.maximum(m, jnp.max(sp, axis=-1, keepdims=True))
            pc = jnp.exp(sc - m)
            l = jnp.sum(pc, axis=-1, keepdims=True)
            o = jnp.dot(pc.astype(BF16), vc, preferred_element_type=F32)
            if n_prev:
                pp = jnp.exp(sp - m)
                l = l + jnp.sum(pp, axis=-1, keepdims=True)
                o = o + jnp.dot(pp.astype(BF16), vp, preferred_element_type=F32)
            o_ref[r0 : r0 + SB, lanes] = (o * (1.0 / l)).astype(o_ref.dtype)


def _attention(proj, k_past, v_past, past_b0, bias, out_prev, row0, B, S, dims):
    ret_qk, ret_v, att_w, dk, hd = dims
    T = proj.shape[0]
    H = att_w // hd
    C = min(CHUNK, S)
    SB = bias.shape[1]
    R = bias.shape[2] - SB
    fresh = k_past is None
    rows = R if fresh else S
    assert S % rows == 0 and row0 % rows == 0 and rows % SB == 0 and SB % C == 0
    nblk = S // rows
    blk0 = row0 // rows
    qcol = (2 * ret_qk + 2 * ret_v) // att_w
    assert (2 * ret_qk + 2 * ret_v) % att_w == 0

    def cur(col):
        return pl.BlockSpec((rows, att_w), lambda b, i: (blk0 + b * nblk + i, col))

    if fresh:
        def prev(col):
            return pl.BlockSpec((rows, att_w), lambda b, i: (blk0 + b * nblk + jnp.maximum(i - 1, 0), col))

        kp_spec, vp_spec = prev(qcol + 1), prev(qcol + 2)
        kp_arg, vp_arg = proj, proj
    else:
        kp_spec = pl.BlockSpec((1, R, att_w), lambda b, i: (past_b0 + b, 0, 0))
        vp_spec = kp_spec
        kp_arg, vp_arg = k_past, v_past
    in_specs = [cur(qcol), kp_spec, cur(qcol + 1), vp_spec, cur(qcol + 2),
                pl.BlockSpec((H, SB, R + SB), lambda b, i: (0, 0, 0), pipeline_mode=pl.Buffered(1))]
    args = [proj, kp_arg, proj, vp_arg, proj, bias]
    aliases = {}
    if out_prev is not None:
        in_specs.append(pl.BlockSpec(memory_space=pl.ANY))
        args.append(out_prev)
        aliases = {len(args) - 1: 0}
    return pl.pallas_call(
        functools.partial(_attention_kernel, C, R, SB // C, H, hd, fresh),
        out_shape=jax.ShapeDtypeStruct((T, att_w), BF16),
        grid=(B, nblk),
        in_specs=in_specs,
        out_specs=pl.BlockSpec((rows, att_w), lambda b, i: (blk0 + b * nblk + i, 0)),
        input_output_aliases=aliases,
        compiler_params=_params("parallel", "arbitrary"),
        name="attention",
    )(*args)


def _merge_kernel(ret_ref, wr_ref, att_ref, wa_ref, gr_ref, ga_ref, o_ref):
    y_ret = jnp.dot(ret_ref[...], wr_ref[...], preferred_element_type=F32)
    y_att = jnp.dot(att_ref[...], wa_ref[...], preferred_element_type=F32)
    merged = gr_ref[...].astype(F32) * y_ret + ga_ref[...].astype(F32) * y_att
    o_ref[...] = merged.astype(o_ref.dtype)


def _merge(gated, att, proj, w_ret_o, w_att_o):
    T, ret_v = gated.shape
    att_w = att.shape[1]
    D = w_ret_o.shape[1]
    in_cols = proj.shape[1]
    tm = _tile(T, 256)
    tn = _tile(D, 1024)
    gcol = (in_cols - 2 * D) // tn
    assert (in_cols - 2 * D) % tn == 0
    return pl.pallas_call(
        _merge_kernel,
        out_shape=jax.ShapeDtypeStruct((T, D), BF16),
        grid=(D // tn, T // tm),
        in_specs=[
            pl.BlockSpec((tm, ret_v), lambda j, i: (i, 0)),
            pl.BlockSpec((ret_v, tn), lambda j, i: (0, j)),
            pl.BlockSpec((tm, att_w), lambda j, i: (i, 0)),
            pl.BlockSpec((att_w, tn), lambda j, i: (0, j)),
            pl.BlockSpec((tm, tn), lambda j, i: (i, gcol + j)),
            pl.BlockSpec((tm, tn), lambda j, i: (i, gcol + D // tn + j)),
        ],
        out_specs=pl.BlockSpec((tm, tn), lambda j, i: (i, j)),
        compiler_params=_params("parallel", "parallel"),
        name="merge",
    )(gated, w_ret_o, att, w_att_o, proj, proj)


def _outproj_kernel(NG, NE, m_ref, w_ref, x_ref, nw_ref, wr_ref, br_ref, x1_ref, h2_ref, ids_ref,
                    gates_ref):
    x1 = x_ref[...] + jnp.dot(m_ref[...], w_ref[...], preferred_element_type=F32)
    x1_ref[...] = x1
    ms = jnp.mean(x1 * x1, axis=-1, keepdims=True)
    h2 = (x1 * lax.rsqrt(ms + EPS)) * nw_ref[...]
    h2_ref[...] = h2
    h_hi = h2.astype(BF16)
    h_lo = (h2 - h_hi.astype(F32)).astype(BF16)
    r_hi = jnp.dot(h_hi, wr_ref[...], preferred_element_type=F32)
    r_lo = jnp.dot(h_lo, wr_ref[:, :LANES], preferred_element_type=F32)
    logits = (r_hi[:, :LANES] + r_hi[:, LANES:]) + r_lo + br_ref[...]
    epg = NE // NG
    lane = lax.broadcasted_iota(jnp.int32, logits.shape, 1).astype(F32)
    big = float(LANES)
    gmask = lane < NG
    gl = jnp.where(gmask, logits, NEG_INF)
    gm = jnp.max(gl, axis=-1, keepdims=True)
    gsel = jnp.min(jnp.where(gl == gm, lane, big), axis=-1, keepdims=True)
    pg = 1.0 / jnp.sum(jnp.where(gmask, jnp.exp(gl - gm), 0.0), axis=-1, keepdims=True)
    lo = NG + gsel * epg
    ingrp = (lane >= lo) & (lane < lo + epg)
    el = jnp.where(ingrp, logits, NEG_INF)
    v1 = jnp.max(el, axis=-1, keepdims=True)
    i1 = jnp.min(jnp.where(el == v1, lane, big), axis=-1, keepdims=True)
    el2 = jnp.where(lane == i1, NEG_INF, el)
    v2 = jnp.max(el2, axis=-1, keepdims=True)
    i2 = jnp.min(jnp.where(el2 == v2, lane, big), axis=-1, keepdims=True)
    t = jnp.exp(v2 - v1)
    den = 1.0 / (1.0 + t)
    g1 = pg * den
    g2 = pg * (t * den)
    ids = jnp.where(lane == 0.0, i1 - NG, jnp.where(lane == 1.0, i2 - NG, 0.0))
    ids_ref[...] = ids.astype(jnp.int32)
    gates_ref[...] = jnp.where(lane == 0.0, g1, jnp.where(lane == 1.0, g2, 0.0))


def _outproj(merged, w_out, x, nw, wr, br, NG, NE):
    T, D = x.shape
    tm = _tile(T, 256)
    row = lambda i: (i, 0)
    const = lambda i: (0, 0)
    return pl.pallas_call(
        functools.partial(_outproj_kernel, NG, NE),
        out_shape=(
            jax.ShapeDtypeStruct((T, D), F32),
            jax.ShapeDtypeStruct((T, D), F32),
            jax.ShapeDtypeStruct((T, LANES), jnp.int32),
            jax.ShapeDtypeStruct((T, LANES), F32),
        ),
        grid=(T // tm,),
        in_specs=[
            pl.BlockSpec((tm, D), row),
            pl.BlockSpec((D, D), const),
            pl.BlockSpec((tm, D), row),
            pl.BlockSpec((1, D), const),
            pl.BlockSpec((D, 2 * LANES), const),
            pl.BlockSpec((1, LANES), const),
        ],
        out_specs=(
            pl.BlockSpec((tm, D), row),
            pl.BlockSpec((tm, D), row),
            pl.BlockSpec((tm, LANES), row),
            pl.BlockSpec((tm, LANES), row),
        ),
        compiler_params=_params("parallel"),
        name="outproj_router",
    )(merged, w_out, x, nw.reshape(1, D), wr, br)


def _slot_tables(ids, NE, blk, NB):
    e_flat = ids[:, :TOP_K].reshape(-1)
    onehot = (e_flat[:, None] == jnp.arange(NE, dtype=jnp.int32)[None, :]).astype(jnp.int32)
    csum = jnp.cumsum(onehot, axis=0)
    counts = csum[-1]
    rank = jnp.sum(onehot * csum, axis=1) - 1
    pcounts = (counts + blk - 1) // blk * blk
    pends = jnp.cumsum(pcounts)
    pstarts = pends - pcounts
    slots = jnp.sum(onehot * pstarts[None, :], axis=1) + rank
    blk_start = jnp.arange(NB, dtype=jnp.int32) * blk
    block_e = jnp.minimum(jnp.sum((pends[None, :] <= blk_start[:, None]).astype(jnp.int32), axis=1), NE - 1)
    n_used = (pends[-1] // blk).reshape(1)
    return slots.astype(jnp.int32), block_e.astype(jnp.int32), n_used.astype(jnp.int32)


def _row_copy(src_ref, src_row, dst_ref, dst_row, sem):
    return pltpu.make_async_copy(src_ref.at[pl.ds(src_row, 1)], dst_ref.at[pl.ds(dst_row, 1)], sem)


def _dispatch_kernel(tm, slots_ref, h_ref, init_ref, buf_ref, sem):
    del init_ref
    base = pl.program_id(0) * (TOP_K * tm)

    def start(r, carry):
        for kk in range(TOP_K):
            _row_copy(h_ref, r, buf_ref, slots_ref[base + TOP_K * r + kk], sem).start()
        return carry

    lax.fori_loop(0, tm, start, 0)

    def wait(r, carry):
        for kk in range(TOP_K):
            _row_copy(h_ref, 0, buf_ref, 0, sem).wait()
        return carry

    lax.fori_loop(0, tm, wait, 0)


def _dispatch(h2, slots, P):
    T, D = h2.shape
    tm = _tile(T, 512)
    init = jnp.zeros((P, D), F32)
    grid_spec = pltpu.PrefetchScalarGridSpec(
        num_scalar_prefetch=1,
        grid=(T // tm,),
        in_specs=[
            pl.BlockSpec((tm, D), lambda i, s: (i, 0)),
            pl.BlockSpec(memory_space=pl.ANY),
        ],
        out_specs=pl.BlockSpec(memory_space=pl.ANY),
        scratch_shapes=[pltpu.SemaphoreType.DMA(())],
    )
    return pl.pallas_call(
        functools.partial(_dispatch_kernel, tm),
        out_shape=jax.ShapeDtypeStruct((P, D), F32),
        grid_spec=grid_spec,
        input_output_aliases={2: 0},
        compiler_params=_params("arbitrary"),
        name="moe_dispatch",
    )(slots, h2, init)


def _experts_kernel(block_e_ref, n_used_ref, x_ref, wu_ref, wd_ref, o_ref):
    i = pl.program_id(0)

    @pl.when(i < n_used_ref[0])
    def _():
        x = x_ref[...].astype(BF16)
        up = jnp.dot(x, wu_ref[0], preferred_element_type=F32)
        ff = up.shape[1] // 2
        a = up[:, :ff]
        b = up[:, ff:]
        hmid = (a * jax.nn.sigmoid(a) * b).astype(BF16)
        o_ref[...] = jnp.dot(hmid, wd_ref[0], preferred_element_type=F32)

    @pl.when(i >= n_used_ref[0])
    def _():
        o_ref[...] = jnp.zeros_like(o_ref)


def _experts(buf, block_e, n_used, w_up, w_down, blk):
    P, D = buf.shape
    NB = P // blk
    ff2 = w_up.shape[2]
    grid_spec = pltpu.PrefetchScalarGridSpec(
        num_scalar_prefetch=2,
        grid=(NB,),
        in_specs=[
            pl.BlockSpec((blk, D), lambda i, be, nu: (i, 0)),
            pl.BlockSpec((1, D, ff2), lambda i, be, nu: (be[i], 0, 0)),
            pl.BlockSpec((1, ff2 // 2, D), lambda i, be, nu: (be[i], 0, 0)),
        ],
        out_specs=pl.BlockSpec((blk, D), lambda i, be, nu: (i, 0)),
    )
    return pl.pallas_call(
        _experts_kernel,
        out_shape=jax.ShapeDtypeStruct((P, D), F32),
        grid_spec=grid_spec,
        compiler_params=_params("arbitrary"),
        name="moe_experts",
    )(block_e, n_used, buf, w_up, w_down)


def _combine_kernel(tm, slots_ref, x_ref, gates_ref, y_ref, o_ref, rows, sem):
    i = pl.program_id(0)
    n = pl.num_programs(0)

    def fetch(tile, b):
        base = tile * (TOP_K * tm)

        def start(r, carry):
            for kk in range(TOP_K):
                _row_copy(y_ref, slots_ref[base + TOP_K * r + kk], rows.at[b, kk], r, sem.at[b]).start()
            return carry

        lax.fori_loop(0, tm, start, 0)

    @pl.when(i == 0)
    def _():
        fetch(0, 0)

    @pl.when(i + 1 < n)
    def _():
        fetch(i + 1, (i + 1) % 2)

    b = i % 2

    def wait(r, carry):
        for kk in range(TOP_K):
            _row_copy(y_ref, 0, rows.at[b, kk], 0, sem.at[b]).wait()
        return carry

    lax.fori_loop(0, tm, wait, 0)
    g = gates_ref[...]
    y = g[:, 0:1] * rows[b, 0]
    for kk in range(1, TOP_K):
        y = y + g[:, kk : kk + 1] * rows[b, kk]
    o_ref[...] = x_ref[...] + y


def _combine(x1, ybuf, slots, gates):
    T, D = x1.shape
    tm = _tile(T, 256)
    grid_spec = pltpu.PrefetchScalarGridSpec(
        num_scalar_prefetch=1,
        grid=(T // tm,),
        in_specs=[
            pl.BlockSpec((tm, D), lambda i, s: (i, 0)),
            pl.BlockSpec((tm, LANES), lambda i, s: (i, 0)),
            pl.BlockSpec(memory_space=pl.ANY),
        ],
        out_specs=pl.BlockSpec((tm, D), lambda i, s: (i, 0)),
        scratch_shapes=[pltpu.VMEM((2, TOP_K, tm, D), F32), pltpu.SemaphoreType.DMA((2,))],
    )
    return pl.pallas_call(
        functools.partial(_combine_kernel, tm),
        out_shape=jax.ShapeDtypeStruct((T, D), F32),
        grid_spec=grid_spec,
        compiler_params=_params("arbitrary"),
        name="moe_combine",
    )(slots, x1, gates, ybuf)


def _rope_tables(Bp, Sp, Bs, Ss, half):
    freqs = ROPE_BASE ** (-jnp.arange(half, dtype=F32) / half)
    pos = jnp.concatenate([jnp.tile(jnp.arange(Sp), Bp), jnp.tile(PAST_LEN + jnp.arange(Ss), Bs)])
    ang = pos.astype(F32)[:, None] * freqs[None, :]
    return jnp.cos(ang), jnp.sin(ang)


def _band_bias(rel_table, C, R, SB):
    H = rel_table.shape[0]
    max_rel = (rel_table.shape[1] - 1) // 2
    L = R + 2 * SB
    t = np.arange(L)
    diag_idx = np.clip(R + SB - 1 - t, -max_rel, max_rel) + max_rel
    diag = jnp.take(rel_table.astype(F32), jnp.asarray(diag_idx, jnp.int32), axis=1)
    flat = jnp.tile(diag, (1, SB))
    toep = flat[:, SB - 1 : SB - 1 + SB * (L - 1)].reshape(H, SB, L - 1)[:, :, : R + SB]
    qc = np.arange(SB)[:, None] // C
    kc = np.arange(R + SB)[None, :] // C
    inband = (kc >= qc) & (kc <= qc + R // C)
    return jnp.where(jnp.asarray(inband)[None], toep, NEG_INF)


def kernel(x_prompt, x_sample, state_ret, cache_att_k, cache_att_v, norm_mix_w, w_in, ret_gn_w,
           w_ret_o, q_norm_w, k_norm_w, rel_bias, w_att_o, b_gate, w_out, norm_ffn_w, w_router_g,
           b_router_g, w_router_e, b_router_e, w_up, w_down):
    Bp, Sp, D = x_prompt.shape
    Bs, Ss, _ = x_sample.shape
    depth = w_in.shape[0]
    _, _, H_ret, dk, dv = state_ret.shape
    _, _, R, H_att, hd = cache_att_k.shape
    ret_qk, ret_v, att_w = H_ret * dk, H_ret * dv, H_att * hd
    dims = (ret_qk, ret_v, att_w, dk, hd)
    NG, NE = w_router_g.shape[2], w_router_e.shape[2]
    Tp, Ts = Bp * Sp, Bs * Ss
    T = Tp + Ts
    keep = min(R, Sp)
    assert NG + NE <= LANES and R % CHUNK == 0

    blk = 256
    NB = (T * TOP_K + NE * (blk - 1) + blk - 1) // blk
    C_p = 256 if Sp % 256 == 0 else min(CHUNK, Sp)
    G_p = max(1, min(512, Sp) // C_p)
    C_s = min(CHUNK, Ss)

    x = jnp.concatenate([x_prompt.reshape(Tp, D), x_sample.reshape(Ts, D)], axis=0)
    cos, sin = _rope_tables(Bp, Sp, Bs, Ss, dk // 2)
    zero_state = jnp.zeros((Bp, H_ret, dk, dv), F32)
    state_all = state_ret.reshape(depth * Bs, H_ret, dk, dv)
    past_k = cache_att_k.reshape(depth * Bs, R, att_w)
    past_v = cache_att_v.reshape(depth * Bs, R, att_w)
    kcol = 2 * ret_qk + 2 * ret_v + att_w
    vcol = kcol + att_w
    C_att = min(CHUNK, Sp)
    SB_p = 4 * C_att if R % (4 * C_att) == 0 else C_att

    rs_p, rs_s, k_p, v_p, k_s, v_s = [], [], [], [], [], []
    for l in range(depth):
        ep = jnp.zeros((1, w_in.shape[2]), F32)
        ep = ep.at[0, kcol - att_w : kcol].set(jnp.tile(q_norm_w[l] * (float(hd) ** -0.5), H_att))
        ep = ep.at[0, kcol:vcol].set(jnp.tile(k_norm_w[l], H_att))
        ep = ep.at[0, vcol + att_w :].set(b_gate[l].reshape(-1))
        wr = jnp.zeros((D, LANES), F32).at[:, :NG].set(w_router_g[l]).at[:, NG : NG + NE].set(w_router_e[l])
        wr_hi = wr.astype(BF16)
        wr_split = jnp.concatenate([wr_hi, (wr - wr_hi.astype(F32)).astype(BF16)], axis=1)
        br = jnp.zeros((1, LANES), F32).at[0, :NG].set(b_router_g[l]).at[0, NG : NG + NE].set(b_router_e[l])
        bias_p = _band_bias(rel_bias[l], C_att, R, SB_p)
        bias_s = _band_bias(rel_bias[l], min(CHUNK, Ss), R, min(CHUNK, Ss))

        h = _rmsnorm(x, norm_mix_w[l])
        proj = _inproj(h, w_in[l].astype(BF16), cos, sin, ep, dims)

        gated, st_p = _retention(proj, zero_state, 0, ret_gn_w[l], None, 0, Bp, Sp, C_p, G_p, dims)
        gated, st_s = _retention(proj, state_all, l * Bs, ret_gn_w[l], gated, Tp, Bs, Ss, C_s, 1, dims)
        att = _attention(proj, None, None, 0, bias_p, None, 0, Bp, Sp, dims)
        att = _attention(proj, past_k, past_v, l * Bs, bias_s, att, Tp, Bs, Ss, dims)

        merged = _merge(gated, att, proj, w_ret_o[l].astype(BF16), w_att_o[l].astype(BF16))
        x1, h2, ids, gates = _outproj(merged, w_out[l].astype(BF16), x, norm_ffn_w[l], wr_split, br, NG, NE)

        slots, block_e, n_used = _slot_tables(ids, NE, blk, NB)
        buf = _dispatch(h2, slots, NB * blk)
        ybuf = _experts(buf, block_e, n_used, w_up[l].astype(BF16), w_down[l].astype(BF16), blk)
        x = _combine(x1, ybuf, slots, gates)

        rs_p.append(st_p)
        rs_s.append(st_s)
        kv_p = jnp.stack([lax.slice(proj, ((b + 1) * Sp - keep, kcol), ((b + 1) * Sp, vcol + att_w))
                          for b in range(Bp)]).astype(F32)
        kv_s = lax.slice(proj, (Tp, kcol), (T, vcol + att_w)).astype(F32).reshape(Bs, Ss, 2 * att_w)
        k_p.append(kv_p[..., :att_w].reshape(Bp, keep, H_att, hd))
        v_p.append(kv_p[..., att_w:].reshape(Bp, keep, H_att, hd))
        k_s.append(kv_s[..., :att_w].reshape(Bs, Ss, H_att, hd))
        v_s.append(kv_s[..., att_w:].reshape(Bs, Ss, H_att, hd))

    y_p = x[:Tp].reshape(Bp, Sp, D)
    y_s = x[Tp:].reshape(Bs, Ss, D)
    return (y_p, y_s, jnp.stack(rs_p), jnp.stack(rs_s), jnp.stack(k_p), jnp.stack(v_p),
            jnp.stack(k_s), jnp.stack(v_s))
```

```python
import functools

import jax
import jax.numpy as jnp
import numpy as np
from jax import lax
from jax.experimental import pallas as pl
from jax.experimental.pallas import tpu as pltpu

F32 = jnp.float32
BF16 = jnp.bfloat16

EPS = 1e-6
NEG_INF = -1e30
ROPE_BASE = 10000.0
PAST_LEN = 1024
CHUNK = 64
TOP_K = 2

V7X_VMEM_LIMIT_BYTES = 56 * 1024 * 1024
LANES = 128
DMA_ISSUE_UNROLL = 8
NT_DIMS = (((1,), (1,)), ((), ()))


def _params(*sem):
    return pltpu.CompilerParams(dimension_semantics=sem, vmem_limit_bytes=V7X_VMEM_LIMIT_BYTES)


def _tile(n, pref):
    t = min(n, pref)
    while n % t:
        t //= 2
    assert t >= 8, (n, pref)
    return t


def _rmsnorm_kernel(x_ref, w_ref, o_ref):
    x = x_ref[...]
    ms = jnp.mean(x * x, axis=-1, keepdims=True)
    o_ref[...] = ((x * lax.rsqrt(ms + EPS)) * w_ref[...]).astype(o_ref.dtype)


def _rmsnorm(x, w):
    T, D = x.shape
    tm = _tile(T, 512)
    return pl.pallas_call(
        _rmsnorm_kernel,
        out_shape=jax.ShapeDtypeStruct((T, D), BF16),
        grid=(T // tm,),
        in_specs=[pl.BlockSpec((tm, D), lambda i: (i, 0)), pl.BlockSpec((1, D), lambda i: (0, 0))],
        out_specs=pl.BlockSpec((tm, D), lambda i: (i, 0)),
        compiler_params=_params("parallel"),
        name="rmsnorm",
    )(x, w.reshape(1, D))


def _proj_kernel(kind, sub, head, h_ref, w_ref, *rest):
    o_ref = rest[-1]
    tm, tn = o_ref.shape
    for k in range(tm // sub):
        rows = slice(k * sub, (k + 1) * sub)
        acc = jnp.dot(h_ref[rows, :], w_ref[...], preferred_element_type=F32)
        if kind == "rotary":
            cos_ref, sin_ref, scale_ref = rest[:3]
            cos = cos_ref[rows, :]
            sin = sin_ref[rows, :]
            half = head // 2
            for hh in range(tn // head):
                lo = slice(hh * head, hh * head + half)
                hi = slice(hh * head + half, (hh + 1) * head)
                t1 = acc[:, lo]
                t2 = acc[:, hi]
                o_ref[rows, lo] = ((t1 * cos - t2 * sin) * scale_ref[:, lo]).astype(o_ref.dtype)
                o_ref[rows, hi] = ((t1 * sin + t2 * cos) * scale_ref[:, hi]).astype(o_ref.dtype)
        elif kind == "norm":
            gain_ref = rest[0]
            for hh in range(tn // head):
                cols = slice(hh * head, (hh + 1) * head)
                t = acc[:, cols]
                ms = jnp.mean(t * t, axis=-1, keepdims=True)
                o_ref[rows, cols] = ((t * lax.rsqrt(ms + EPS)) * gain_ref[:, cols]).astype(o_ref.dtype)
        elif kind == "gate":
            o_ref[rows, :] = jax.nn.sigmoid(acc + rest[0][...]).astype(o_ref.dtype)
        else:
            o_ref[rows, :] = acc.astype(o_ref.dtype)


def _proj(kind, h, w, head=None, vec=None, cos=None, sin=None):
    T, D = h.shape
    N = w.shape[1]
    tm = _tile(T, 1024)
    tn = _tile(N, 1024)
    sub = _tile(tm, 256)
    assert head is None or tn % head == 0
    in_specs = [pl.BlockSpec((tm, D), lambda i, j: (i, 0)), pl.BlockSpec((D, tn), lambda i, j: (0, j))]
    args = [h, w]
    if kind == "rotary":
        in_specs += [pl.BlockSpec((tm, head // 2), lambda i, j: (i, 0))] * 2
        args += [cos, sin]
    if vec is not None:
        in_specs.append(pl.BlockSpec((1, tn), lambda i, j: (0, j)))
        args.append(vec)
    return pl.pallas_call(
        functools.partial(_proj_kernel, kind, sub, head),
        out_shape=jax.ShapeDtypeStruct((T, N), BF16),
        grid=(T // tm, N // tn),
        in_specs=in_specs,
        out_specs=pl.BlockSpec((tm, tn), lambda i, j: (i, j)),
        compiler_params=_params("parallel", "arbitrary"),
        name="inproj_" + kind,
    )(*args)


def _retention_kernel(C, G, q_ref, k_ref, v_ref, g_ref, s0_ref, intra_ref, cross_ref, sdec_ref,
                      cdec_ref, gnw_ref, *rest):
    o_ref, s_ref = rest[-2], rest[-1]
    n = pl.program_id(2)

    @pl.when(n == 0)
    def _():
        s_ref[0, 0] = s0_ref[0, 0]

    def body(g, carry):
        r0 = pl.multiple_of(g * C, C)
        q = q_ref[pl.ds(r0, C), :]
        k = k_ref[pl.ds(r0, C), :]
        v = v_ref[pl.ds(r0, C), :]
        s = s_ref[0, 0]
        scores = lax.dot_general(q, k, NT_DIMS, preferred_element_type=F32) * intra_ref[0]
        o = jnp.dot(scores.astype(BF16), v, preferred_element_type=F32)
        o = o + jnp.dot(q, s.astype(BF16), preferred_element_type=F32) * cross_ref[0]
        kd_t = (k.astype(F32) * sdec_ref[0]).T.astype(BF16)
        s_ref[0, 0] = cdec_ref[0] * s + jnp.dot(kd_t, v, preferred_element_type=F32)
        mu = jnp.mean(o, axis=-1, keepdims=True)
        d = o - mu
        var = jnp.mean(d * d, axis=-1, keepdims=True)
        y = d * lax.rsqrt(var + EPS) * gnw_ref[0]
        gate = g_ref[pl.ds(r0, C), :].astype(F32)
        o_ref[pl.ds(r0, C), :] = (gate * jax.nn.sigmoid(gate) * y).astype(o_ref.dtype)
        return carry

    lax.fori_loop(0, G, body, 0)


def _decay_tables(H, C):
    log_gamma = jnp.log1p(-jnp.exp2(-5.0 - jnp.arange(H, dtype=F32)))
    idx = jnp.arange(C, dtype=F32)
    diff = idx[:, None] - idx[None, :]
    intra = jnp.where(diff >= 0, jnp.exp(log_gamma[:, None, None] * jnp.maximum(diff, 0.0)), 0.0)
    cross = jnp.exp(log_gamma[:, None] * (idx + 1.0))[:, :, None]
    sdec = jnp.exp(log_gamma[:, None] * (C - 1.0 - idx))[:, :, None]
    cdec = jnp.exp(log_gamma * C)[:, None, None]
    return intra, cross, sdec, cdec


def _retention(rqk, rvg, s0, s0_b0, gnw, out_prev, row0, B, S, C, G, dims):
    ret_qk, ret_v, att_w, dk, hd = dims
    T = rqk.shape[0]
    H = ret_qk // dk
    dv = ret_v // H
    rows = C * G
    nblk = S // rows
    assert S % rows == 0 and row0 % rows == 0
    blk0 = row0 // rows
    intra, cross, sdec, cdec = _decay_tables(H, C)
    cdec = jnp.broadcast_to(cdec, (H, 1, dv))
    kcol, vcol, gcol = H, 0, H

    def rowmap(col0):
        return lambda b, h, n: (blk0 + b * nblk + n, col0 + h)

    in_specs = [
        pl.BlockSpec((rows, dk), rowmap(0)),
        pl.BlockSpec((rows, dk), rowmap(kcol)),
        pl.BlockSpec((rows, dv), rowmap(vcol)),
        pl.BlockSpec((rows, dv), rowmap(gcol)),
        pl.BlockSpec((1, 1, dk, dv), lambda b, h, n: (s0_b0 + b, h, 0, 0)),
        pl.BlockSpec((1, C, C), lambda b, h, n: (h, 0, 0)),
        pl.BlockSpec((1, C, 1), lambda b, h, n: (h, 0, 0)),
        pl.BlockSpec((1, C, 1), lambda b, h, n: (h, 0, 0)),
        pl.BlockSpec((1, 1, dv), lambda b, h, n: (h, 0, 0)),
        pl.BlockSpec((1, 1, dv), lambda b, h, n: (h, 0, 0)),
    ]
    args = [rqk, rqk, rvg, rvg, s0, intra, cross, sdec, cdec, gnw.reshape(H, 1, dv)]
    aliases = {}
    if out_prev is not None:
        in_specs.append(pl.BlockSpec(memory_space=pl.ANY))
        args.append(out_prev)
        aliases = {len(args) - 1: 0}
    return pl.pallas_call(
        functools.partial(_retention_kernel, C, G),
        out_shape=(jax.ShapeDtypeStruct((T, ret_v), BF16), jax.ShapeDtypeStruct((B, H, dk, dv), F32)),
        grid=(B, H, nblk),
        in_specs=in_specs,
        out_specs=(
            pl.BlockSpec((rows, dv), rowmap(0)),
            pl.BlockSpec((1, 1, dk, dv), lambda b, h, n: (b, h, 0, 0)),
        ),
        input_output_aliases=aliases,
        compiler_params=_params("parallel", "parallel", "arbitrary"),
        name="retention",
    )(*args)


def _attention_kernel(C, R, NC, H, hd, mask_first, q_ref, kp_ref, kc_ref, vp_ref, vc_ref, bias_ref,
                      *rest):
    o_ref = rest[-1]
    i = pl.program_id(1)
    SB = NC * C
    prev3d = len(kp_ref.shape) == 3
    prev_off = jnp.where(i > 0, 0.0, NEG_INF) if mask_first else None

    for sb in range(q_ref.shape[0] // SB):
        r0 = sb * SB
        n_prev = max(R - r0, 0)
        c0 = max(r0 - R, 0)
        for h in range(H):
            lanes = slice(h * hd, (h + 1) * hd)
            q = q_ref[r0 : r0 + SB, lanes]
            kc = kc_ref[c0 : r0 + SB, lanes]
            vc = vc_ref[c0 : r0 + SB, lanes]
            sc = lax.dot_general(q, kc, NT_DIMS, preferred_element_type=F32) + bias_ref[h, :, n_prev:]
            m = jnp.max(sc, axis=-1, keepdims=True)
            if n_prev:
                if prev3d:
                    kp = kp_ref[0, r0:R, lanes].astype(BF16)
                    vp = vp_ref[0, r0:R, lanes].astype(BF16)
                else:
                    kp = kp_ref[r0:R, lanes]
                    vp = vp_ref[r0:R, lanes]
                sp = lax.dot_general(q, kp, NT_DIMS, preferred_element_type=F32) + bias_ref[h, :, :n_prev]
                if mask_first:
                    sp = sp + prev_off
                m = jnp.maximum(m, jnp.max(sp, axis=-1, keepdims=True))
            pc = jnp.exp(sc - m)
            l = jnp.sum(pc, axis=-1, keepdims=True)
            o = jnp.dot(pc.astype(BF16), vc, preferred_element_type=F32)
            if n_prev:
                pp = jnp.exp(sp - m)
                l = l + jnp.sum(pp, axis=-1, keepdims=True)
                o = o + jnp.dot(pp.astype(BF16), vp, preferred_element_type=F32)
            o_ref[r0 : r0 + SB, lanes] = (o * (1.0 / l)).astype(o_ref.dtype)


def _attention(aqk, av, k_past, v_past, past_b0, bias, out_prev, row0, B, S, dims):
    ret_qk, ret_v, att_w, dk, hd = dims
    T = aqk.shape[0]
    H = att_w // hd
    C = min(CHUNK, S)
    SB = bias.shape[1]
    R = bias.shape[2] - SB
    fresh = k_past is None
    rows = R if fresh else S
    assert S % rows == 0 and row0 % rows == 0 and rows % SB == 0 and SB % C == 0
    nblk = S // rows
    blk0 = row0 // rows

    def cur(col):
        return pl.BlockSpec((rows, att_w), lambda b, i: (blk0 + b * nblk + i, col))

    if fresh:
        def prev(col):
            return pl.BlockSpec((rows, att_w), lambda b, i: (blk0 + b * nblk + jnp.maximum(i - 1, 0), col))

        kp_spec, vp_spec = prev(1), prev(0)
        kp_arg, vp_arg = aqk, av
    else:
        kp_spec = pl.BlockSpec((1, R, att_w), lambda b, i: (past_b0 + b, 0, 0))
        vp_spec = kp_spec
        kp_arg, vp_arg = k_past, v_past
    in_specs = [cur(0), kp_spec, cur(1), vp_spec, cur(0),
                pl.BlockSpec((H, SB, R + SB), lambda b, i: (0, 0, 0), pipeline_mode=pl.Buffered(1))]
    args = [aqk, kp_arg, aqk, vp_arg, av, bias]
    aliases = {}
    if out_prev is not None:
        in_specs.append(pl.BlockSpec(memory_space=pl.ANY))
        args.append(out_prev)
        aliases = {len(args) - 1: 0}
    return pl.pallas_call(
        functools.partial(_attention_kernel, C, R, SB // C, H, hd, fresh),
        out_shape=jax.ShapeDtypeStruct((T, att_w), BF16),
        grid=(B, nblk),
        in_specs=in_specs,
        out_specs=pl.BlockSpec((rows, att_w), lambda b, i: (blk0 + b * nblk + i, 0)),
        input_output_aliases=aliases,
        compiler_params=_params("parallel", "arbitrary"),
        name="attention",
    )(*args)


def _merge_kernel(ret_ref, wr_ref, att_ref, wa_ref, gr_ref, ga_ref, o_ref):
    y_ret = jnp.dot(ret_ref[...], wr_ref[...], preferred_element_type=F32)
    y_att = jnp.dot(att_ref[...], wa_ref[...], preferred_element_type=F32)
    merged = gr_ref[...].astype(F32) * y_ret + ga_ref[...].astype(F32) * y_att
    o_ref[...] = merged.astype(o_ref.dtype)


def _merge(gated, att, sg, w_ret_o, w_att_o):
    T, ret_v = gated.shape
    att_w = att.shape[1]
    D = w_ret_o.shape[1]
    tm = _tile(T, 256)
    tn = _tile(D, 1024)
    gcol = 0
    return pl.pallas_call(
        _merge_kernel,
        out_shape=jax.ShapeDtypeStruct((T, D), BF16),
        grid=(D // tn, T // tm),
        in_specs=[
            pl.BlockSpec((tm, ret_v), lambda j, i: (i, 0)),
            pl.BlockSpec((ret_v, tn), lambda j, i: (0, j)),
            pl.BlockSpec((tm, att_w), lambda j, i: (i, 0)),
            pl.BlockSpec((att_w, tn), lambda j, i: (0, j)),
            pl.BlockSpec((tm, tn), lambda j, i: (i, gcol + j)),
            pl.BlockSpec((tm, tn), lambda j, i: (i, gcol + D // tn + j)),
        ],
        out_specs=pl.BlockSpec((tm, tn), lambda j, i: (i, j)),
        compiler_params=_params("parallel", "parallel"),
        name="merge",
    )(gated, w_ret_o, att, w_att_o, sg, sg)


def _outproj_kernel(NG, NE, m_ref, w_ref, x_ref, nw_ref, wr_ref, br_ref, x1_ref, h2_ref, ids_ref,
                    gates_ref):
    x1 = x_ref[...] + jnp.dot(m_ref[...], w_ref[...], preferred_element_type=F32)
    x1_ref[...] = x1
    ms = jnp.mean(x1 * x1, axis=-1, keepdims=True)
    h2 = (x1 * lax.rsqrt(ms + EPS)) * nw_ref[...]
    h2_ref[...] = h2
    h_hi = h2.astype(BF16)
    h_lo = (h2 - h_hi.astype(F32)).astype(BF16)
    r_hi = jnp.dot(h_hi, wr_ref[...], preferred_element_type=F32)
    r_lo = jnp.dot(h_lo, wr_ref[:, :LANES], preferred_element_type=F32)
    logits = (r_hi[:, :LANES] + r_hi[:, LANES:]) + r_lo + br_ref[...]
    epg = NE // NG
    lane = lax.broadcasted_iota(jnp.int32, logits.shape, 1).astype(F32)
    big = float(LANES)
    gmask = lane < NG
    gl = jnp.where(gmask, logits, NEG_INF)
    gm = jnp.max(gl, axis=-1, keepdims=True)
    gsel = jnp.min(jnp.where(gl == gm, lane, big), axis=-1, keepdims=True)
    pg = 1.0 / jnp.sum(jnp.where(gmask, jnp.exp(gl - gm), 0.0), axis=-1, keepdims=True)
    lo = NG + gsel * epg
    ingrp = (lane >= lo) & (lane < lo + epg)
    el = jnp.where(ingrp, logits, NEG_INF)
    v1 = jnp.max(el, axis=-1, keepdims=True)
    i1 = jnp.min(jnp.where(el == v1, lane, big), axis=-1, keepdims=True)
    el2 = jnp.where(lane == i1, NEG_INF, el)
    v2 = jnp.max(el2, axis=-1, keepdims=True)
    i2 = jnp.min(jnp.where(el2 == v2, lane, big), axis=-1, keepdims=True)
    t = jnp.exp(v2 - v1)
    den = 1.0 / (1.0 + t)
    g1 = pg * den
    g2 = pg * (t * den)
    ids = jnp.where(lane == 0.0, i1 - NG, jnp.where(lane == 1.0, i2 - NG, 0.0))
    ids_ref[...] = ids.astype(jnp.int32)
    gates_ref[...] = jnp.where(lane == 0.0, g1, jnp.where(lane == 1.0, g2, 0.0))


def _outproj(merged, w_out, x, nw, wr, br, NG, NE):
    T, D = x.shape
    tm = _tile(T, 256)
    row = lambda i: (i, 0)
    const = lambda i: (0, 0)
    return pl.pallas_call(
        functools.partial(_outproj_kernel, NG, NE),
        out_shape=(
            jax.ShapeDtypeStruct((T, D), F32),
            jax.ShapeDtypeStruct((T, D), F32),
            jax.ShapeDtypeStruct((T, LANES), jnp.int32),
            jax.ShapeDtypeStruct((T, LANES), F32),
        ),
        grid=(T // tm,),
        in_specs=[
            pl.BlockSpec((tm, D), row),
            pl.BlockSpec((D, D), const),
            pl.BlockSpec((tm, D), row),
            pl.BlockSpec((1, D), const),
            pl.BlockSpec((D, 2 * LANES), const),
            pl.BlockSpec((1, LANES), const),
        ],
        out_specs=(
            pl.BlockSpec((tm, D), row),
            pl.BlockSpec((tm, D), row),
            pl.BlockSpec((tm, LANES), row),
            pl.BlockSpec((tm, LANES), row),
        ),
        compiler_params=_params("parallel"),
        name="outproj_router",
    )(merged, w_out, x, nw.reshape(1, D), wr, br)


def _slot_tables(ids, NE, blk, NB):
    e_flat = ids[:, :TOP_K].reshape(-1)
    onehot = (e_flat[:, None] == jnp.arange(NE, dtype=jnp.int32)[None, :]).astype(jnp.int32)
    csum = jnp.cumsum(onehot, axis=0)
    counts = csum[-1]
    rank = jnp.sum(onehot * csum, axis=1) - 1
    pcounts = (counts + blk - 1) // blk * blk
    pends = jnp.cumsum(pcounts)
    pstarts = pends - pcounts
    slots = jnp.sum(onehot * pstarts[None, :], axis=1) + rank
    blk_start = jnp.arange(NB, dtype=jnp.int32) * blk
    block_e = jnp.minimum(jnp.sum((pends[None, :] <= blk_start[:, None]).astype(jnp.int32), axis=1), NE - 1)
    n_used = (pends[-1] // blk).reshape(1)
    return slots.astype(jnp.int32), block_e.astype(jnp.int32), n_used.astype(jnp.int32)


def _row_copy(src_ref, src_row, dst_ref, dst_row, sem):
    return pltpu.make_async_copy(src_ref.at[pl.ds(src_row, 1)], dst_ref.at[pl.ds(dst_row, 1)], sem)


def _dispatch_kernel(tm, slots_ref, h_ref, init_ref, buf_ref, sem):
    del init_ref
    base = pl.program_id(0) * (TOP_K * tm)

    def start(r, carry):
        for kk in range(TOP_K):
            _row_copy(h_ref, r, buf_ref, slots_ref[base + TOP_K * r + kk], sem).start()
        return carry

    lax.fori_loop(0, tm, start, 0, unroll=DMA_ISSUE_UNROLL)
    for kk in range(TOP_K):
        pltpu.make_async_copy(h_ref, buf_ref.at[pl.ds(0, tm)], sem).wait()


def _dispatch(h2, slots, P):
    T, D = h2.shape
    tm = _tile(T, 512)
    init = jnp.zeros((P, D), F32)
    grid_spec = pltpu.PrefetchScalarGridSpec(
        num_scalar_prefetch=1,
        grid=(T // tm,),
        in_specs=[
            pl.BlockSpec((tm, D), lambda i, s: (i, 0)),
            pl.BlockSpec(memory_space=pl.ANY),
        ],
        out_specs=pl.BlockSpec(memory_space=pl.ANY),
        scratch_shapes=[pltpu.SemaphoreType.DMA(())],
    )
    return pl.pallas_call(
        functools.partial(_dispatch_kernel, tm),
        out_shape=jax.ShapeDtypeStruct((P, D), F32),
        grid_spec=grid_spec,
        input_output_aliases={2: 0},
        compiler_params=_params("arbitrary"),
        name="moe_dispatch",
    )(slots, h2, init)


def _experts_kernel(block_e_ref, n_used_ref, x_ref, wu_ref, wd_ref, o_ref):
    i = pl.program_id(0)

    @pl.when(i < n_used_ref[0])
    def _():
        x = x_ref[...].astype(BF16)
        up = jnp.dot(x, wu_ref[0], preferred_element_type=F32)
        ff = up.shape[1] // 2
        a = up[:, :ff]
        b = up[:, ff:]
        hmid = (a * jax.nn.sigmoid(a) * b).astype(BF16)
        o_ref[...] = jnp.dot(hmid, wd_ref[0], preferred_element_type=F32)

    @pl.when(i >= n_used_ref[0])
    def _():
        o_ref[...] = jnp.zeros_like(o_ref)


def _experts(buf, block_e, n_used, w_up, w_down, blk):
    P, D = buf.shape
    NB = P // blk
    ff2 = w_up.shape[2]
    grid_spec = pltpu.PrefetchScalarGridSpec(
        num_scalar_prefetch=2,
        grid=(NB,),
        in_specs=[
            pl.BlockSpec((blk, D), lambda i, be, nu: (i, 0)),
            pl.BlockSpec((1, D, ff2), lambda i, be, nu: (be[i], 0, 0)),
            pl.BlockSpec((1, ff2 // 2, D), lambda i, be, nu: (be[i], 0, 0)),
        ],
        out_specs=pl.BlockSpec((blk, D), lambda i, be, nu: (i, 0)),
    )
    return pl.pallas_call(
        _experts_kernel,
        out_shape=jax.ShapeDtypeStruct((P, D), F32),
        grid_spec=grid_spec,
        compiler_params=_params("arbitrary"),
        name="moe_experts",
    )(block_e, n_used, buf, w_up, w_down)


def _combine_kernel(tm, with_norm, slots_ref, x_ref, gates_ref, y_ref, *rest):
    if with_norm:
        nw_ref, o_ref, h_ref, rows, sem = rest
    else:
        o_ref, rows, sem = rest
    i = pl.program_id(0)
    n = pl.num_programs(0)

    def fetch(tile, b):
        base = tile * (TOP_K * tm)

        def start(r, carry):
            for kk in range(TOP_K):
                _row_copy(y_ref, slots_ref[base + TOP_K * r + kk], rows.at[b, kk], r, sem.at[b]).start()
            return carry

        lax.fori_loop(0, tm, start, 0, unroll=DMA_ISSUE_UNROLL)

    @pl.when(i == 0)
    def _():
        fetch(0, 0)

    @pl.when(i + 1 < n)
    def _():
        fetch(i + 1, (i + 1) % 2)

    b = i % 2
    for kk in range(TOP_K):
        pltpu.make_async_copy(y_ref.at[pl.ds(0, tm)], rows.at[b, kk], sem.at[b]).wait()
    g = gates_ref[...]
    y = g[:, 0:1] * rows[b, 0]
    for kk in range(1, TOP_K):
        y = y + g[:, kk : kk + 1] * rows[b, kk]
    out = x_ref[...] + y
    o_ref[...] = out
    if with_norm:
        ms = jnp.mean(out * out, axis=-1, keepdims=True)
        h_ref[...] = ((out * lax.rsqrt(ms + EPS)) * nw_ref[...]).astype(h_ref.dtype)


def _combine(x1, ybuf, slots, gates, next_norm_w):
    T, D = x1.shape
    tm = _tile(T, 256)
    with_norm = next_norm_w is not None
    row = pl.BlockSpec((tm, D), lambda i, s: (i, 0))
    in_specs = [row, pl.BlockSpec((tm, LANES), lambda i, s: (i, 0)), pl.BlockSpec(memory_space=pl.ANY)]
    args = [slots, x1, gates, ybuf]
    out_shape = jax.ShapeDtypeStruct((T, D), F32)
    out_specs = row
    if with_norm:
        in_specs.append(pl.BlockSpec((1, D), lambda i, s: (0, 0)))
        args.append(next_norm_w.reshape(1, D))
        out_shape = (out_shape, jax.ShapeDtypeStruct((T, D), BF16))
        out_specs = (row, row)
    grid_spec = pltpu.PrefetchScalarGridSpec(
        num_scalar_prefetch=1,
        grid=(T // tm,),
        in_specs=in_specs,
        out_specs=out_specs,
        scratch_shapes=[pltpu.VMEM((2, TOP_K, tm, D), F32), pltpu.SemaphoreType.DMA((2,))],
    )
    return pl.pallas_call(
        functools.partial(_combine_kernel, tm, with_norm),
        out_shape=out_shape,
        grid_spec=grid_spec,
        compiler_params=_params("arbitrary"),
        name="moe_combine",
    )(*args)


def _rope_tables(Bp, Sp, Bs, Ss, half):
    freqs = ROPE_BASE ** (-jnp.arange(half, dtype=F32) / half)
    pos = jnp.concatenate([jnp.tile(jnp.arange(Sp), Bp), jnp.tile(PAST_LEN + jnp.arange(Ss), Bs)])
    ang = pos.astype(F32)[:, None] * freqs[None, :]
    return jnp.cos(ang), jnp.sin(ang)


def _band_bias(rel_table, C, R, SB):
    H = rel_table.shape[0]
    max_rel = (rel_table.shape[1] - 1) // 2
    L = R + 2 * SB
    t = np.arange(L)
    diag_idx = np.clip(R + SB - 1 - t, -max_rel, max_rel) + max_rel
    diag = jnp.take(rel_table.astype(F32), jnp.asarray(diag_idx, jnp.int32), axis=1)
    flat = jnp.tile(diag, (1, SB))
    toep = flat[:, SB - 1 : SB - 1 + SB * (L - 1)].reshape(H, SB, L - 1)[:, :, : R + SB]
    qc = np.arange(SB)[:, None] // C
    kc = np.arange(R + SB)[None, :] // C
    inband = (kc >= qc) & (kc <= qc + R // C)
    return jnp.where(jnp.asarray(inband)[None], toep, NEG_INF)


def kernel(x_prompt, x_sample, state_ret, cache_att_k, cache_att_v, norm_mix_w, w_in, ret_gn_w,
           w_ret_o, q_norm_w, k_norm_w, rel_bias, w_att_o, b_gate, w_out, norm_ffn_w, w_router_g,
           b_router_g, w_router_e, b_router_e, w_up, w_down):
    Bp, Sp, D = x_prompt.shape
    Bs, Ss, _ = x_sample.shape
    depth = w_in.shape[0]
    _, _, H_ret, dk, dv = state_ret.shape
    _, _, R, H_att, hd = cache_att_k.shape
    ret_qk, ret_v, att_w = H_ret * dk, H_ret * dv, H_att * hd
    dims = (ret_qk, ret_v, att_w, dk, hd)
    NG, NE = w_router_g.shape[2], w_router_e.shape[2]
    Tp, Ts = Bp * Sp, Bs * Ss
    T = Tp + Ts
    keep = min(R, Sp)
    assert NG + NE <= LANES and R % CHUNK == 0

    blk = 256
    NB = (T * TOP_K + NE * (blk - 1) + blk - 1) // blk
    C_p = 256 if Sp % 256 == 0 else min(CHUNK, Sp)
    G_p = max(1, min(1024, Sp) // C_p)
    C_s = min(CHUNK, Ss)

    x = jnp.concatenate([x_prompt.reshape(Tp, D), x_sample.reshape(Ts, D)], axis=0)
    cos, sin = _rope_tables(Bp, Sp, Bs, Ss, dk // 2)
    zero_state = jnp.zeros((Bp, H_ret, dk, dv), F32)
    state_all = state_ret.reshape(depth * Bs, H_ret, dk, dv)
    past_k = cache_att_k.reshape(depth * Bs, R, att_w)
    past_v = cache_att_v.reshape(depth * Bs, R, att_w)
    C_att = min(CHUNK, Sp)
    SB_p = 4 * C_att if R % (4 * C_att) == 0 else C_att
    c1 = 2 * ret_qk
    c2 = c1 + 2 * ret_v
    c3 = c2 + 2 * att_w
    c4 = c3 + att_w
    rot_scale = jnp.concatenate([jnp.ones((1, ret_qk), F32), jnp.full((1, ret_qk), float(dk) ** -0.5, F32)], axis=1)

    def mixer_input(l, h):
        w = w_in[l]
        qk_gain = jnp.concatenate([jnp.tile(q_norm_w[l] * (float(hd) ** -0.5), H_att),
                                   jnp.tile(k_norm_w[l], H_att)]).reshape(1, 2 * att_w)
        rqk = _proj("rotary", h, w[:, :c1].astype(BF16), head=dk, vec=rot_scale, cos=cos, sin=sin)
        rvg = _proj("copy", h, w[:, c1:c2].astype(BF16))
        aqk = _proj("norm", h, w[:, c2:c3].astype(BF16), head=hd, vec=qk_gain)
        av = _proj("copy", h, w[:, c3:c4].astype(BF16))
        sg = _proj("gate", h, w[:, c4:].astype(BF16), vec=b_gate[l].reshape(1, 2 * D))
        return rqk, rvg, aqk, av, sg

    rs_p, rs_s, k_p, v_p, k_s, v_s = [], [], [], [], [], []
    h = _rmsnorm(x, norm_mix_w[0])
    for l in range(depth):
        wr = jnp.zeros((D, LANES), F32).at[:, :NG].set(w_router_g[l]).at[:, NG : NG + NE].set(w_router_e[l])
        wr_hi = wr.astype(BF16)
        wr_split = jnp.concatenate([wr_hi, (wr - wr_hi.astype(F32)).astype(BF16)], axis=1)
        br = jnp.zeros((1, LANES), F32).at[0, :NG].set(b_router_g[l]).at[0, NG : NG + NE].set(b_router_e[l])
        bias_p = _band_bias(rel_bias[l], C_att, R, SB_p)
        bias_s = _band_bias(rel_bias[l], min(CHUNK, Ss), R, min(CHUNK, Ss))

        rqk, rvg, aqk, av, sg = mixer_input(l, h)

        gated, st_p = _retention(rqk, rvg, zero_state, 0, ret_gn_w[l], None, 0, Bp, Sp, C_p, G_p, dims)
        gated, st_s = _retention(rqk, rvg, state_all, l * Bs, ret_gn_w[l], gated, Tp, Bs, Ss, C_s, 1, dims)
        att = _attention(aqk, av, None, None, 0, bias_p, None, 0, Bp, Sp, dims)
        att = _attention(aqk, av, past_k, past_v, l * Bs, bias_s, att, Tp, Bs, Ss, dims)

        merged = _merge(gated, att, sg, w_ret_o[l].astype(BF16), w_att_o[l].astype(BF16))
        x1, h2, ids, gates = _outproj(merged, w_out[l].astype(BF16), x, norm_ffn_w[l], wr_split, br, NG, NE)

        slots, block_e, n_used = _slot_tables(ids, NE, blk, NB)
        buf = _dispatch(h2, slots, NB * blk)
        ybuf = _experts(buf, block_e, n_used, w_up[l].astype(BF16), w_down[l].astype(BF16), blk)
        if l + 1 < depth:
            x, h = _combine(x1, ybuf, slots, gates, norm_mix_w[l + 1])
        else:
            x = _combine(x1, ybuf, slots, gates, None)

        rs_p.append(st_p)
        rs_s.append(st_s)

        def newest(a, lo, hi):
            rows_p = jnp.stack([lax.slice(a, ((b + 1) * Sp - keep, lo), ((b + 1) * Sp, hi)) for b in range(Bp)])
            rows_s = lax.slice(a, (Tp, lo), (T, hi))
            return (rows_p.astype(F32).reshape(Bp, keep, H_att, hd), rows_s.astype(F32).reshape(Bs, Ss, H_att, hd))

        kp_l, ks_l = newest(aqk, att_w, 2 * att_w)
        vp_l, vs_l = newest(av, 0, att_w)
        k_p.append(kp_l)
        v_p.append(vp_l)
        k_s.append(ks_l)
        v_s.append(vs_l)

    y_p = x[:Tp].reshape(Bp, Sp, D)
    y_s = x[Tp:].reshape(Bs, Ss, D)
    return (y_p, y_s, jnp.stack(rs_p), jnp.stack(rs_s), jnp.stack(k_p), jnp.stack(v_p),
            jnp.stack(k_s), jnp.stack(v_s))
```

```python
import functools

import jax
import jax.numpy as jnp
import numpy as np
from jax import lax
from jax.experimental import pallas as pl
from jax.experimental.pallas import tpu as pltpu

F32 = jnp.float32
BF16 = jnp.bfloat16

EPS = 1e-6
NEG_INF = -1e30
ROPE_BASE = 10000.0
PAST_LEN = 1024
CHUNK = 64
TOP_K = 2

V7X_VMEM_LIMIT_BYTES = 56 * 1024 * 1024
LANES = 128
DMA_ISSUE_UNROLL = 8
NT_DIMS = (((1,), (1,)), ((), ()))


def _params(*sem):
    return pltpu.CompilerParams(dimension_semantics=sem, vmem_limit_bytes=V7X_VMEM_LIMIT_BYTES)


def _tile(n, pref):
    t = min(n, pref)
    while n % t:
        t //= 2
    assert t >= 8, (n, pref)
    return t


def _rmsnorm_kernel(x_ref, w_ref, o_ref):
    x = x_ref[...]
    ms = jnp.mean(x * x, axis=-1, keepdims=True)
    o_ref[...] = ((x * lax.rsqrt(ms + EPS)) * w_ref[...]).astype(o_ref.dtype)


def _rmsnorm(x, w):
    T, D = x.shape
    tm = _tile(T, 512)
    return pl.pallas_call(
        _rmsnorm_kernel,
        out_shape=jax.ShapeDtypeStruct((T, D), BF16),
        grid=(T // tm,),
        in_specs=[pl.BlockSpec((tm, D), lambda i: (i, 0)), pl.BlockSpec((1, D), lambda i: (0, 0))],
        out_specs=pl.BlockSpec((tm, D), lambda i: (i, 0)),
        compiler_params=_params("parallel"),
        name="rmsnorm",
    )(x, w.reshape(1, D))


def _proj_kernel(kind, sub, head, h_ref, w_ref, *rest):
    o_ref = rest[-1]
    tm, tn = o_ref.shape
    for k in range(tm // sub):
        rows = slice(k * sub, (k + 1) * sub)
        acc = jnp.dot(h_ref[rows, :], w_ref[...], preferred_element_type=F32)
        if kind == "rotary":
            cos_ref, sin_ref, scale_ref = rest[:3]
            cos = cos_ref[rows, :]
            sin = sin_ref[rows, :]
            half = head // 2
            for hh in range(tn // head):
                lo = slice(hh * head, hh * head + half)
                hi = slice(hh * head + half, (hh + 1) * head)
                t1 = acc[:, lo]
                t2 = acc[:, hi]
                o_ref[rows, lo] = ((t1 * cos - t2 * sin) * scale_ref[:, lo]).astype(o_ref.dtype)
                o_ref[rows, hi] = ((t1 * sin + t2 * cos) * scale_ref[:, hi]).astype(o_ref.dtype)
        elif kind == "norm":
            gain_ref = rest[0]
            for hh in range(tn // head):
                cols = slice(hh * head, (hh + 1) * head)
                t = acc[:, cols]
                ms = jnp.mean(t * t, axis=-1, keepdims=True)
                o_ref[rows, cols] = ((t * lax.rsqrt(ms + EPS)) * gain_ref[:, cols]).astype(o_ref.dtype)
        elif kind == "gate":
            o_ref[rows, :] = jax.nn.sigmoid(acc + rest[0][...]).astype(o_ref.dtype)
        else:
            o_ref[rows, :] = acc.astype(o_ref.dtype)


def _proj(kind, h, w, head=None, vec=None, cos=None, sin=None):
    T, D = h.shape
    N = w.shape[1]
    tm = _tile(T, 1024)
    tn = _tile(N, 1024)
    sub = _tile(tm, 256)
    assert head is None or tn % head == 0
    in_specs = [pl.BlockSpec((tm, D), lambda i, j: (i, 0)), pl.BlockSpec((D, tn), lambda i, j: (0, j))]
    args = [h, w]
    if kind == "rotary":
        in_specs += [pl.BlockSpec((tm, head // 2), lambda i, j: (i, 0))] * 2
        args += [cos, sin]
    if vec is not None:
        in_specs.append(pl.BlockSpec((1, tn), lambda i, j: (0, j)))
        args.append(vec)
    return pl.pallas_call(
        functools.partial(_proj_kernel, kind, sub, head),
        out_shape=jax.ShapeDtypeStruct((T, N), BF16),
        grid=(T // tm, N // tn),
        in_specs=in_specs,
        out_specs=pl.BlockSpec((tm, tn), lambda i, j: (i, j)),
        compiler_params=_params("parallel", "arbitrary"),
        name="inproj_" + kind,
    )(*args)


def _retention_kernel(C, G, q_ref, k_ref, v_ref, g_ref, s0_ref, intra_ref, cross_ref, sdec_ref,
                      cdec_ref, gnw_ref, *rest):
    o_ref, s_ref = rest[-2], rest[-1]
    n = pl.program_id(2)
    HP = s_ref.shape[1]
    dk = s_ref.shape[2]
    dv = s_ref.shape[3]

    @pl.when(n == 0)
    def _():
        s_ref[...] = s0_ref[...]

    def body(g, carry):
        r0 = pl.multiple_of(g * C, C)
        for hh in range(HP):
            qk_cols = slice(hh * dk, (hh + 1) * dk)
            v_cols = slice(hh * dv, (hh + 1) * dv)
            q = q_ref[pl.ds(r0, C), qk_cols]
            k = k_ref[pl.ds(r0, C), qk_cols]
            v = v_ref[pl.ds(r0, C), v_cols]
            s = s_ref[0, hh]
            scores = lax.dot_general(q, k, NT_DIMS, preferred_element_type=F32) * intra_ref[hh]
            o = jnp.dot(scores.astype(BF16), v, preferred_element_type=F32)
            o = o + jnp.dot(q, s.astype(BF16), preferred_element_type=F32) * cross_ref[hh]
            kd_t = (k.astype(F32) * sdec_ref[hh]).T.astype(BF16)
            s_ref[0, hh] = cdec_ref[hh] * s + jnp.dot(kd_t, v, preferred_element_type=F32)
            mu = jnp.mean(o, axis=-1, keepdims=True)
            d = o - mu
            var = jnp.mean(d * d, axis=-1, keepdims=True)
            y = d * lax.rsqrt(var + EPS) * gnw_ref[hh]
            gate = g_ref[pl.ds(r0, C), v_cols].astype(F32)
            o_ref[pl.ds(r0, C), v_cols] = (gate * jax.nn.sigmoid(gate) * y).astype(o_ref.dtype)
        return carry

    lax.fori_loop(0, G, body, 0)


def _decay_tables(H, C):
    log_gamma = jnp.log1p(-jnp.exp2(-5.0 - jnp.arange(H, dtype=F32)))
    idx = jnp.arange(C, dtype=F32)
    diff = idx[:, None] - idx[None, :]
    intra = jnp.where(diff >= 0, jnp.exp(log_gamma[:, None, None] * jnp.maximum(diff, 0.0)), 0.0)
    cross = jnp.exp(log_gamma[:, None] * (idx + 1.0))[:, :, None]
    sdec = jnp.exp(log_gamma[:, None] * (C - 1.0 - idx))[:, :, None]
    cdec = jnp.exp(log_gamma * C)[:, None, None]
    return intra, cross, sdec, cdec


def _retention(rqk, rvg, s0, s0_b0, gnw, out_prev, row0, B, S, C, G, dims):
    ret_qk, ret_v, att_w, dk, hd = dims
    T = rqk.shape[0]
    H = ret_qk // dk
    dv = ret_v // H
    rows = C * G
    nblk = S // rows
    assert S % rows == 0 and row0 % rows == 0
    blk0 = row0 // rows
    intra, cross, sdec, cdec = _decay_tables(H, C)
    cdec = jnp.broadcast_to(cdec, (H, 1, dv))
    vcol = 0

    HP = 2 if H % 2 == 0 else 1
    kcol, gcol = H // HP, H // HP

    def rowmap(col0):
        return lambda b, h, n: (blk0 + b * nblk + n, col0 + h)

    per_head = lambda b, h, n: (h, 0, 0)
    in_specs = [
        pl.BlockSpec((rows, HP * dk), rowmap(0)),
        pl.BlockSpec((rows, HP * dk), rowmap(kcol)),
        pl.BlockSpec((rows, HP * dv), rowmap(vcol)),
        pl.BlockSpec((rows, HP * dv), rowmap(gcol)),
        pl.BlockSpec((1, HP, dk, dv), lambda b, h, n: (s0_b0 + b, h, 0, 0)),
        pl.BlockSpec((HP, C, C), per_head),
        pl.BlockSpec((HP, C, 1), per_head),
        pl.BlockSpec((HP, C, 1), per_head),
        pl.BlockSpec((HP, 1, dv), per_head),
        pl.BlockSpec((HP, 1, dv), per_head),
    ]
    args = [rqk, rqk, rvg, rvg, s0, intra, cross, sdec, cdec, gnw.reshape(H, 1, dv)]
    aliases = {}
    if out_prev is not None:
        in_specs.append(pl.BlockSpec(memory_space=pl.ANY))
        args.append(out_prev)
        aliases = {len(args) - 1: 0}
    return pl.pallas_call(
        functools.partial(_retention_kernel, C, G),
        out_shape=(jax.ShapeDtypeStruct((T, ret_v), BF16), jax.ShapeDtypeStruct((B, H, dk, dv), F32)),
        grid=(B, H // HP, nblk),
        in_specs=in_specs,
        out_specs=(
            pl.BlockSpec((rows, HP * dv), rowmap(0)),
            pl.BlockSpec((1, HP, dk, dv), lambda b, h, n: (b, h, 0, 0)),
        ),
        input_output_aliases=aliases,
        compiler_params=_params("parallel", "parallel", "arbitrary"),
        name="retention",
    )(*args)


def _attention_kernel(C, R, NC, H, hd, mask_first, q_ref, kp_ref, kc_ref, vp_ref, vc_ref, bias_ref,
                      *rest):
    o_ref = rest[-1]
    i = pl.program_id(1)
    SB = NC * C
    prev_cache = len(kp_ref.shape) == 5
    prev_off = jnp.where(i > 0, 0.0, NEG_INF) if mask_first else None

    for sb in range(q_ref.shape[0] // SB):
        r0 = sb * SB
        n_prev = max(R - r0, 0)
        c0 = max(r0 - R, 0)
        for h in range(H):
            lanes = slice(h * hd, (h + 1) * hd)
            q = q_ref[r0 : r0 + SB, lanes]
            kc = kc_ref[c0 : r0 + SB, lanes]
            vc = vc_ref[c0 : r0 + SB, lanes]
            sc = lax.dot_general(q, kc, NT_DIMS, preferred_element_type=F32) + bias_ref[h, :, n_prev:]
            m = jnp.max(sc, axis=-1, keepdims=True)
            if n_prev:
                if prev_cache:
                    kp = kp_ref[0, 0, r0:R, h, :].astype(BF16)
                    vp = vp_ref[0, 0, r0:R, h, :].astype(BF16)
                else:
                    kp = kp_ref[r0:R, lanes]
                    vp = vp_ref[r0:R, lanes]
                sp = lax.dot_general(q, kp, NT_DIMS, preferred_element_type=F32) + bias_ref[h, :, :n_prev]
                if mask_first:
                    sp = sp + prev_off
                m = jnp.maximum(m, jnp.max(sp, axis=-1, keepdims=True))
            pc = jnp.exp(sc - m)
            l = jnp.sum(pc, axis=-1, keepdims=True)
            o = jnp.dot(pc.astype(BF16), vc, preferred_element_type=F32)
            if n_prev:
                pp = jnp.exp(sp - m)
                l = l + jnp.sum(pp, axis=-1, keepdims=True)
                o = o + jnp.dot(pp.astype(BF16), vp, preferred_element_type=F32)
            o_ref[r0 : r0 + SB, lanes] = (o * (1.0 / l)).astype(o_ref.dtype)


def _attention(aqk, av, k_past, v_past, past_layer, bias, out_prev, row0, B, S, dims):
    ret_qk, ret_v, att_w, dk, hd = dims
    T = aqk.shape[0]
    H = att_w // hd
    C = min(CHUNK, S)
    SB = bias.shape[1]
    R = bias.shape[2] - SB
    fresh = k_past is None
    rows = R if fresh else S
    assert S % rows == 0 and row0 % rows == 0 and rows % SB == 0 and SB % C == 0
    nblk = S // rows
    blk0 = row0 // rows

    def cur(col):
        return pl.BlockSpec((rows, att_w), lambda b, i: (blk0 + b * nblk + i, col))

    if fresh:
        def prev(col):
            return pl.BlockSpec((rows, att_w), lambda b, i: (blk0 + b * nblk + jnp.maximum(i - 1, 0), col))

        kp_spec, vp_spec = prev(1), prev(0)
        kp_arg, vp_arg = aqk, av
    else:
        kp_spec = pl.BlockSpec((1, 1, R, H, hd), lambda b, i: (past_layer, b, 0, 0, 0))
        vp_spec = kp_spec
        kp_arg, vp_arg = k_past, v_past
    in_specs = [cur(0), kp_spec, cur(1), vp_spec, cur(0),
                pl.BlockSpec((H, SB, R + SB), lambda b, i: (0, 0, 0), pipeline_mode=pl.Buffered(1))]
    args = [aqk, kp_arg, aqk, vp_arg, av, bias]
    aliases = {}
    if out_prev is not None:
        in_specs.append(pl.BlockSpec(memory_space=pl.ANY))
        args.append(out_prev)
        aliases = {len(args) - 1: 0}
    return pl.pallas_call(
        functools.partial(_attention_kernel, C, R, SB // C, H, hd, fresh),
        out_shape=jax.ShapeDtypeStruct((T, att_w), BF16),
        grid=(B, nblk),
        in_specs=in_specs,
        out_specs=pl.BlockSpec((rows, att_w), lambda b, i: (blk0 + b * nblk + i, 0)),
        input_output_aliases=aliases,
        compiler_params=_params("parallel", "arbitrary"),
        name="attention",
    )(*args)


def _merge_kernel(ret_ref, wr_ref, att_ref, wa_ref, gr_ref, ga_ref, o_ref):
    y_ret = jnp.dot(ret_ref[...], wr_ref[...], preferred_element_type=F32)
    y_att = jnp.dot(att_ref[...], wa_ref[...], preferred_element_type=F32)
    merged = gr_ref[...].astype(F32) * y_ret + ga_ref[...].astype(F32) * y_att
    o_ref[...] = merged.astype(o_ref.dtype)


def _merge(gated, att, sg, w_ret_o, w_att_o):
    T, ret_v = gated.shape
    att_w = att.shape[1]
    D = w_ret_o.shape[1]
    tm = _tile(T, 256)
    tn = _tile(D, 1024)
    gcol = 0
    return pl.pallas_call(
        _merge_kernel,
        out_shape=jax.ShapeDtypeStruct((T, D), BF16),
        grid=(D // tn, T // tm),
        in_specs=[
            pl.BlockSpec((tm, ret_v), lambda j, i: (i, 0)),
            pl.BlockSpec((ret_v, tn), lambda j, i: (0, j)),
            pl.BlockSpec((tm, att_w), lambda j, i: (i, 0)),
            pl.BlockSpec((att_w, tn), lambda j, i: (0, j)),
            pl.BlockSpec((tm, tn), lambda j, i: (i, gcol + j)),
            pl.BlockSpec((tm, tn), lambda j, i: (i, gcol + D // tn + j)),
        ],
        out_specs=pl.BlockSpec((tm, tn), lambda j, i: (i, j)),
        compiler_params=_params("parallel", "parallel"),
        name="merge",
    )(gated, w_ret_o, att, w_att_o, sg, sg)


def _outproj_kernel(NG, NE, m_ref, w_ref, x_ref, nw_ref, wr_ref, br_ref, x1_ref, h2_ref, ids_ref,
                    gates_ref):
    x1 = x_ref[...] + jnp.dot(m_ref[...], w_ref[...], preferred_element_type=F32)
    x1_ref[...] = x1
    ms = jnp.mean(x1 * x1, axis=-1, keepdims=True)
    h2 = (x1 * lax.rsqrt(ms + EPS)) * nw_ref[...]
    h2_ref[...] = h2
    h_hi = h2.astype(BF16)
    h_lo = (h2 - h_hi.astype(F32)).astype(BF16)
    r_hi = jnp.dot(h_hi, wr_ref[...], preferred_element_type=F32)
    r_lo = jnp.dot(h_lo, wr_ref[:, :LANES], preferred_element_type=F32)
    logits = (r_hi[:, :LANES] + r_hi[:, LANES:]) + r_lo + br_ref[...]
    epg = NE // NG
    lane = lax.broadcasted_iota(jnp.int32, logits.shape, 1).astype(F32)
    big = float(LANES)
    gmask = lane < NG
    gl = jnp.where(gmask, logits, NEG_INF)
    gm = jnp.max(gl, axis=-1, keepdims=True)
    gsel = jnp.min(jnp.where(gl == gm, lane, big), axis=-1, keepdims=True)
    pg = 1.0 / jnp.sum(jnp.where(gmask, jnp.exp(gl - gm), 0.0), axis=-1, keepdims=True)
    lo = NG + gsel * epg
    ingrp = (lane >= lo) & (lane < lo + epg)
    el = jnp.where(ingrp, logits, NEG_INF)
    v1 = jnp.max(el, axis=-1, keepdims=True)
    i1 = jnp.min(jnp.where(el == v1, lane, big), axis=-1, keepdims=True)
    el2 = jnp.where(lane == i1, NEG_INF, el)
    v2 = jnp.max(el2, axis=-1, keepdims=True)
    i2 = jnp.min(jnp.where(el2 == v2, lane, big), axis=-1, keepdims=True)
    t = jnp.exp(v2 - v1)
    den = 1.0 / (1.0 + t)
    g1 = pg * den
    g2 = pg * (t * den)
    ids = jnp.where(lane == 0.0, i1 - NG, jnp.where(lane == 1.0, i2 - NG, 0.0))
    ids_ref[...] = ids.astype(jnp.int32)
    gates_ref[...] = jnp.where(lane == 0.0, g1, jnp.where(lane == 1.0, g2, 0.0))


def _outproj(merged, w_out, x, nw, wr, br, NG, NE):
    T, D = x.shape
    tm = _tile(T, 256)
    row = lambda i: (i, 0)
    const = lambda i: (0, 0)
    return pl.pallas_call(
        functools.partial(_outproj_kernel, NG, NE),
        out_shape=(
            jax.ShapeDtypeStruct((T, D), F32),
            jax.ShapeDtypeStruct((T, D), F32),
            jax.ShapeDtypeStruct((T, LANES), jnp.int32),
            jax.ShapeDtypeStruct((T, LANES), F32),
        ),
        grid=(T // tm,),
        in_specs=[
            pl.BlockSpec((tm, D), row),
            pl.BlockSpec((D, D), const),
            pl.BlockSpec((tm, D), row),
            pl.BlockSpec((1, D), const),
            pl.BlockSpec((D, 2 * LANES), const),
            pl.BlockSpec((1, LANES), const),
        ],
        out_specs=(
            pl.BlockSpec((tm, D), row),
            pl.BlockSpec((tm, D), row),
            pl.BlockSpec((tm, LANES), row),
            pl.BlockSpec((tm, LANES), row),
        ),
        compiler_params=_params("parallel"),
        name="outproj_router",
    )(merged, w_out, x, nw.reshape(1, D), wr, br)


def _slot_tables(ids, NE, blk, NB):
    e_flat = ids[:, :TOP_K].reshape(-1)
    A = e_flat.shape[0]
    seg = _tile(A, 256)
    onehot = (e_flat[:, None] == jnp.arange(NE, dtype=jnp.int32)[None, :]).reshape(A // seg, seg, NE)
    tril = jnp.tril(jnp.ones((seg, seg), BF16))
    within = jnp.einsum("rc,tce->tre", tril, onehot.astype(BF16), preferred_element_type=F32)
    seg_tot = within[:, -1, :]
    seg_end = jnp.cumsum(seg_tot, axis=0)
    csum = (within + (seg_end - seg_tot)[:, None, :]).astype(jnp.int32)
    counts = seg_end[-1].astype(jnp.int32)
    oh = onehot.astype(jnp.int32)
    rank = jnp.sum(oh * csum, axis=2).reshape(A) - 1
    pcounts = (counts + blk - 1) // blk * blk
    pends = jnp.cumsum(pcounts)
    pstarts = pends - pcounts
    slots = jnp.sum(oh * pstarts[None, None, :], axis=2).reshape(A) + rank
    blk_start = jnp.arange(NB, dtype=jnp.int32) * blk
    block_e = jnp.minimum(jnp.sum((pends[None, :] <= blk_start[:, None]).astype(jnp.int32), axis=1), NE - 1)
    n_used = (pends[-1] // blk).reshape(1)
    pad_rows = jnp.where(pcounts > 0, pends - blk, -1)
    return (slots.astype(jnp.int32), block_e.astype(jnp.int32), n_used.astype(jnp.int32),
            pad_rows.astype(jnp.int32))


def _row_copy(src_ref, src_row, dst_ref, dst_row, sem):
    return pltpu.make_async_copy(src_ref.at[pl.ds(src_row, 1)], dst_ref.at[pl.ds(dst_row, 1)], sem)


def _dispatch_kernel(tm, blk, slots_ref, pad_rows_ref, h_ref, buf_ref, zeros, sem):
    base = pl.program_id(0) * (TOP_K * tm)

    @pl.when(pl.program_id(0) == 0)
    def _():
        zeros[...] = jnp.zeros_like(zeros)
        for e in range(pad_rows_ref.shape[0]):
            @pl.when(pad_rows_ref[e] >= 0)
            def _():
                row0 = pl.multiple_of(pad_rows_ref[e], blk)
                pltpu.make_async_copy(zeros, buf_ref.at[pl.ds(row0, blk)], sem).start()
        for e in range(pad_rows_ref.shape[0]):
            @pl.when(pad_rows_ref[e] >= 0)
            def _():
                pltpu.make_async_copy(zeros, buf_ref.at[pl.ds(0, blk)], sem).wait()

    def start(r, carry):
        for kk in range(TOP_K):
            _row_copy(h_ref, r, buf_ref, slots_ref[base + TOP_K * r + kk], sem).start()
        return carry

    lax.fori_loop(0, tm, start, 0, unroll=DMA_ISSUE_UNROLL)
    for kk in range(TOP_K):
        pltpu.make_async_copy(h_ref, buf_ref.at[pl.ds(0, tm)], sem).wait()


def _dispatch(h2, slots, pad_rows, P, blk):
    T, D = h2.shape
    tm = _tile(T, 512)
    grid_spec = pltpu.PrefetchScalarGridSpec(
        num_scalar_prefetch=2,
        grid=(T // tm,),
        in_specs=[pl.BlockSpec((tm, D), lambda i, s, p: (i, 0))],
        out_specs=pl.BlockSpec(memory_space=pl.ANY),
        scratch_shapes=[pltpu.VMEM((blk, D), F32), pltpu.SemaphoreType.DMA(())],
    )
    return pl.pallas_call(
        functools.partial(_dispatch_kernel, tm, blk),
        out_shape=jax.ShapeDtypeStruct((P, D), F32),
        grid_spec=grid_spec,
        compiler_params=_params("arbitrary"),
        name="moe_dispatch",
    )(slots, pad_rows, h2)


def _experts_kernel(block_e_ref, n_used_ref, x_ref, wu_ref, wd_ref, o_ref):
    i = pl.program_id(0)

    @pl.when(i < n_used_ref[0])
    def _():
        x = x_ref[...].astype(BF16)
        up = jnp.dot(x, wu_ref[0], preferred_element_type=F32)
        ff = up.shape[1] // 2
        a = up[:, :ff]
        b = up[:, ff:]
        hmid = (a * jax.nn.sigmoid(a) * b).astype(BF16)
        o_ref[...] = jnp.dot(hmid, wd_ref[0], preferred_element_type=F32)

    @pl.when(i >= n_used_ref[0])
    def _():
        o_ref[...] = jnp.zeros_like(o_ref)


def _experts(buf, block_e, n_used, w_up, w_down, blk):
    P, D = buf.shape
    NB = P // blk
    ff2 = w_up.shape[2]
    grid_spec = pltpu.PrefetchScalarGridSpec(
        num_scalar_prefetch=2,
        grid=(NB,),
        in_specs=[
            pl.BlockSpec((blk, D), lambda i, be, nu: (jnp.where(i < nu[0], i, 0), 0)),
            pl.BlockSpec((1, D, ff2), lambda i, be, nu: (be[i], 0, 0)),
            pl.BlockSpec((1, ff2 // 2, D), lambda i, be, nu: (be[i], 0, 0)),
        ],
        out_specs=pl.BlockSpec((blk, D), lambda i, be, nu: (i, 0)),
    )
    return pl.pallas_call(
        _experts_kernel,
        out_shape=jax.ShapeDtypeStruct((P, D), F32),
        grid_spec=grid_spec,
        compiler_params=_params("arbitrary"),
        name="moe_experts",
    )(block_e, n_used, buf, w_up, w_down)


def _combine_kernel(tm, with_norm, split_tiles, slots_ref, x_ref, gates_ref, y_ref, *rest):
    if with_norm:
        nw_ref, o_ref, h_ref, rows, sem = rest
    elif split_tiles is not None:
        o_ref, o2_ref, rows, sem = rest
    else:
        o_ref, rows, sem = rest
    i = pl.program_id(0)
    n = pl.num_programs(0)

    def fetch(tile, b):
        base = tile * (TOP_K * tm)

        def start(r, carry):
            for kk in range(TOP_K):
                _row_copy(y_ref, slots_ref[base + TOP_K * r + kk], rows.at[b, kk], r, sem.at[b]).start()
            return carry

        lax.fori_loop(0, tm, start, 0, unroll=DMA_ISSUE_UNROLL)

    @pl.when(i == 0)
    def _():
        fetch(0, 0)

    @pl.when(i + 1 < n)
    def _():
        fetch(i + 1, (i + 1) % 2)

    b = i % 2
    for kk in range(TOP_K):
        pltpu.make_async_copy(y_ref.at[pl.ds(0, tm)], rows.at[b, kk], sem.at[b]).wait()
    g = gates_ref[...]
    y = g[:, 0:1] * rows[b, 0]
    for kk in range(1, TOP_K):
        y = y + g[:, kk : kk + 1] * rows[b, kk]
    out = x_ref[...] + y
    if split_tiles is None:
        o_ref[...] = out
    else:
        @pl.when(i < split_tiles)
        def _():
            o_ref[...] = out

        @pl.when(i >= split_tiles)
        def _():
            o2_ref[...] = out
    if with_norm:
        ms = jnp.mean(out * out, axis=-1, keepdims=True)
        h_ref[...] = ((out * lax.rsqrt(ms + EPS)) * nw_ref[...]).astype(h_ref.dtype)


def _combine(x1, ybuf, slots, gates, next_norm_w=None, split_rows=None):
    T, D = x1.shape
    tm = _tile(T, 256)
    with_norm = next_norm_w is not None
    split_tiles = None
    row = pl.BlockSpec((tm, D), lambda i, s: (i, 0))
    in_specs = [row, pl.BlockSpec((tm, LANES), lambda i, s: (i, 0)), pl.BlockSpec(memory_space=pl.ANY)]
    args = [slots, x1, gates, ybuf]
    out_shape = jax.ShapeDtypeStruct((T, D), F32)
    out_specs = row
    if with_norm:
        in_specs.append(pl.BlockSpec((1, D), lambda i, s: (0, 0)))
        args.append(next_norm_w.reshape(1, D))
        out_shape = (out_shape, jax.ShapeDtypeStruct((T, D), BF16))
        out_specs = (row, row)
    elif split_rows is not None:
        assert split_rows % tm == 0 and 0 < split_rows < T
        split_tiles = split_rows // tm
        out_shape = (jax.ShapeDtypeStruct((split_rows, D), F32), jax.ShapeDtypeStruct((T - split_rows, D), F32))
        out_specs = (pl.BlockSpec((tm, D), lambda i, s: (jnp.minimum(i, split_tiles - 1), 0)),
                     pl.BlockSpec((tm, D), lambda i, s: (jnp.maximum(i - split_tiles, 0), 0)))
    grid_spec = pltpu.PrefetchScalarGridSpec(
        num_scalar_prefetch=1,
        grid=(T // tm,),
        in_specs=in_specs,
        out_specs=out_specs,
        scratch_shapes=[pltpu.VMEM((2, TOP_K, tm, D), F32), pltpu.SemaphoreType.DMA((2,))],
    )
    return pl.pallas_call(
        functools.partial(_combine_kernel, tm, with_norm, split_tiles),
        out_shape=out_shape,
        grid_spec=grid_spec,
        compiler_params=_params("arbitrary"),
        name="moe_combine",
    )(*args)


def _rope_tables(Bp, Sp, Bs, Ss, half):
    freqs = ROPE_BASE ** (-jnp.arange(half, dtype=F32) / half)
    pos = jnp.concatenate([jnp.tile(jnp.arange(Sp), Bp), jnp.tile(PAST_LEN + jnp.arange(Ss), Bs)])
    ang = pos.astype(F32)[:, None] * freqs[None, :]
    return jnp.cos(ang), jnp.sin(ang)


def _band_bias(rel_table, C, R, SB):
    H = rel_table.shape[0]
    max_rel = (rel_table.shape[1] - 1) // 2
    L = R + 2 * SB
    t = np.arange(L)
    diag_idx = np.clip(R + SB - 1 - t, -max_rel, max_rel) + max_rel
    diag = jnp.take(rel_table.astype(F32), jnp.asarray(diag_idx, jnp.int32), axis=1)
    flat = jnp.tile(diag, (1, SB))
    toep = flat[:, SB - 1 : SB - 1 + SB * (L - 1)].reshape(H, SB, L - 1)[:, :, : R + SB]
    qc = np.arange(SB)[:, None] // C
    kc = np.arange(R + SB)[None, :] // C
    inband = (kc >= qc) & (kc <= qc + R // C)
    return jnp.where(jnp.asarray(inband)[None], toep, NEG_INF)


def kernel(x_prompt, x_sample, state_ret, cache_att_k, cache_att_v, norm_mix_w, w_in, ret_gn_w,
           w_ret_o, q_norm_w, k_norm_w, rel_bias, w_att_o, b_gate, w_out, norm_ffn_w, w_router_g,
           b_router_g, w_router_e, b_router_e, w_up, w_down):
    Bp, Sp, D = x_prompt.shape
    Bs, Ss, _ = x_sample.shape
    depth = w_in.shape[0]
    _, _, H_ret, dk, dv = state_ret.shape
    _, _, R, H_att, hd = cache_att_k.shape
    ret_qk, ret_v, att_w = H_ret * dk, H_ret * dv, H_att * hd
    dims = (ret_qk, ret_v, att_w, dk, hd)
    NG, NE = w_router_g.shape[2], w_router_e.shape[2]
    Tp, Ts = Bp * Sp, Bs * Ss
    T = Tp + Ts
    keep = min(R, Sp)
    assert NG + NE <= LANES and R % CHUNK == 0

    blk = 256
    NB = (T * TOP_K + NE * (blk - 1) + blk - 1) // blk
    C_p = 256 if Sp % 256 == 0 else min(CHUNK, Sp)
    G_p = max(1, min(1024, Sp) // C_p)
    C_s = min(CHUNK, Ss)

    x = jnp.concatenate([x_prompt.reshape(Tp, D), x_sample.reshape(Ts, D)], axis=0)
    cos, sin = _rope_tables(Bp, Sp, Bs, Ss, dk // 2)
    zero_state = jnp.zeros((Bp, H_ret, dk, dv), F32)
    state_all = state_ret.reshape(depth * Bs, H_ret, dk, dv)
    C_att = min(CHUNK, Sp)
    SB_p = 4 * C_att if R % (4 * C_att) == 0 else C_att
    c1 = 2 * ret_qk
    c2 = c1 + 2 * ret_v
    c3 = c2 + 2 * att_w
    c4 = c3 + att_w
    rot_scale = jnp.concatenate([jnp.ones((1, ret_qk), F32), jnp.full((1, ret_qk), float(dk) ** -0.5, F32)], axis=1)

    def mixer_input(l, h):
        w = w_in[l]
        qk_gain = jnp.concatenate([jnp.tile(q_norm_w[l] * (float(hd) ** -0.5), H_att),
                                   jnp.tile(k_norm_w[l], H_att)]).reshape(1, 2 * att_w)
        rqk = _proj("rotary", h, w[:, :c1].astype(BF16), head=dk, vec=rot_scale, cos=cos, sin=sin)
        rvg = _proj("copy", h, w[:, c1:c2].astype(BF16))
        aqk = _proj("norm", h, w[:, c2:c3].astype(BF16), head=hd, vec=qk_gain)
        av = _proj("copy", h, w[:, c3:c4].astype(BF16))
        sg = _proj("gate", h, w[:, c4:].astype(BF16), vec=b_gate[l].reshape(1, 2 * D))
        return rqk, rvg, aqk, av, sg

    rs_p, rs_s, k_p, v_p, k_s, v_s = [], [], [], [], [], []
    h = _rmsnorm(x, norm_mix_w[0])
    for l in range(depth):
        wr = jnp.zeros((D, LANES), F32).at[:, :NG].set(w_router_g[l]).at[:, NG : NG + NE].set(w_router_e[l])
        wr_hi = wr.astype(BF16)
        wr_split = jnp.concatenate([wr_hi, (wr - wr_hi.astype(F32)).astype(BF16)], axis=1)
        br = jnp.zeros((1, LANES), F32).at[0, :NG].set(b_router_g[l]).at[0, NG : NG + NE].set(b_router_e[l])
        bias_p = _band_bias(rel_bias[l], C_att, R, SB_p)
        bias_s = _band_bias(rel_bias[l], min(CHUNK, Ss), R, min(CHUNK, Ss))

        rqk, rvg, aqk, av, sg = mixer_input(l, h)

        gated, st_p = _retention(rqk, rvg, zero_state, 0, ret_gn_w[l], None, 0, Bp, Sp, C_p, G_p, dims)
        gated, st_s = _retention(rqk, rvg, state_all, l * Bs, ret_gn_w[l], gated, Tp, Bs, Ss, C_s, 1, dims)
        att = _attention(aqk, av, None, None, 0, bias_p, None, 0, Bp, Sp, dims)
        att = _attention(aqk, av, cache_att_k, cache_att_v, l, bias_s, att, Tp, Bs, Ss, dims)

        merged = _merge(gated, att, sg, w_ret_o[l].astype(BF16), w_att_o[l].astype(BF16))
        x1, h2, ids, gates = _outproj(merged, w_out[l].astype(BF16), x, norm_ffn_w[l], wr_split, br, NG, NE)

        slots, block_e, n_used, pad_rows = _slot_tables(ids, NE, blk, NB)
        buf = _dispatch(h2, slots, pad_rows, NB * blk, blk)
        ybuf = _experts(buf, block_e, n_used, w_up[l].astype(BF16), w_down[l].astype(BF16), blk)
        if l + 1 < depth:
            x, h = _combine(x1, ybuf, slots, gates, next_norm_w=norm_mix_w[l + 1])
        else:
            y_p, y_s = _combine(x1, ybuf, slots, gates, split_rows=Tp)

        rs_p.append(st_p)
        rs_s.append(st_s)

        def newest(a, lo, hi):
            rows_p = jnp.stack([lax.slice(a, ((b + 1) * Sp - keep, lo), ((b + 1) * Sp, hi)) for b in range(Bp)])
            rows_s = lax.slice(a, (Tp, lo), (T, hi))
            return (rows_p.astype(F32).reshape(Bp, keep, H_att, hd), rows_s.astype(F32).reshape(Bs, Ss, H_att, hd))

        kp_l, ks_l = newest(aqk, att_w, 2 * att_w)
        vp_l, vs_l = newest(av, 0, att_w)
        k_p.append(kp_l)
        v_p.append(vp_l)
        k_s.append(ks_l)
        v_s.append(vs_l)

    return (y_p.reshape(Bp, Sp, D), y_s.reshape(Bs, Ss, D), jnp.stack(rs_p), jnp.stack(rs_s), jnp.stack(k_p), jnp.stack(v_p),
            jnp.stack(k_s), jnp.stack(v_s))
```

```python
import functools

import jax
import jax.numpy as jnp
import numpy as np
from jax import lax
from jax.experimental import pallas as pl
from jax.experimental.pallas import tpu as pltpu

F32 = jnp.float32
BF16 = jnp.bfloat16

EPS = 1e-6
NEG_INF = -1e30
ROPE_BASE = 10000.0
PAST_LEN = 1024
CHUNK = 64
TOP_K = 2

V7X_VMEM_LIMIT_BYTES = 56 * 1024 * 1024
LANES = 128
DMA_ISSUE_UNROLL = 8
NT_DIMS = (((1,), (1,)), ((), ()))


def _params(*sem):
    return pltpu.CompilerParams(dimension_semantics=sem, vmem_limit_bytes=V7X_VMEM_LIMIT_BYTES)


def _tile(n, pref):
    t = min(n, pref)
    while n % t:
        t //= 2
    assert t >= 8, (n, pref)
    return t


def _concat_rmsnorm_kernel(split_tiles, xa_ref, xb_ref, w_ref, x_ref, h_ref):
    def emit(src_ref):
        x = src_ref[...]
        x_ref[...] = x
        ms = jnp.mean(x * x, axis=-1, keepdims=True)
        h_ref[...] = ((x * lax.rsqrt(ms + EPS)) * w_ref[...]).astype(h_ref.dtype)

    @pl.when(pl.program_id(0) < split_tiles)
    def _():
        emit(xa_ref)

    @pl.when(pl.program_id(0) >= split_tiles)
    def _():
        emit(xb_ref)


def _concat_rmsnorm(xa, xb, w):
    Ta, D = xa.shape
    Tb = xb.shape[0]
    tm = _tile(Ta, 512)
    while Tb % tm:
        tm //= 2
    na = Ta // tm
    row = pl.BlockSpec((tm, D), lambda i: (i, 0))
    return pl.pallas_call(
        functools.partial(_concat_rmsnorm_kernel, na),
        out_shape=(jax.ShapeDtypeStruct((Ta + Tb, D), F32), jax.ShapeDtypeStruct((Ta + Tb, D), BF16)),
        grid=((Ta + Tb) // tm,),
        in_specs=[
            pl.BlockSpec((tm, D), lambda i: (jnp.minimum(i, na - 1), 0)),
            pl.BlockSpec((tm, D), lambda i: (jnp.maximum(i - na, 0), 0)),
            pl.BlockSpec((1, D), lambda i: (0, 0)),
        ],
        out_specs=(row, row),
        compiler_params=_params("arbitrary"),
        name="concat_rmsnorm",
    )(xa, xb, w.reshape(1, D))


def _proj_kernel(kind, sub, head, h_ref, w_ref, *rest):
    o_ref, w_bf = rest[-2], rest[-1]
    rest = rest[:-1]
    tm, tn = o_ref.shape

    @pl.when(pl.program_id(1) == 0)
    def _():
        w_bf[...] = w_ref[0].astype(BF16)

    for k in range(tm // sub):
        rows = slice(k * sub, (k + 1) * sub)
        acc = jnp.dot(h_ref[rows, :], w_bf[...], preferred_element_type=F32)
        if kind == "rotary":
            cos_ref, sin_ref, scale_ref = rest[:3]
            cos = cos_ref[rows, :]
            sin = sin_ref[rows, :]
            half = head // 2
            for hh in range(tn // head):
                lo = slice(hh * head, hh * head + half)
                hi = slice(hh * head + half, (hh + 1) * head)
                t1 = acc[:, lo]
                t2 = acc[:, hi]
                o_ref[rows, lo] = ((t1 * cos - t2 * sin) * scale_ref[:, lo]).astype(o_ref.dtype)
                o_ref[rows, hi] = ((t1 * sin + t2 * cos) * scale_ref[:, hi]).astype(o_ref.dtype)
        elif kind == "norm":
            gain_ref = rest[0]
            for hh in range(tn // head):
                cols = slice(hh * head, (hh + 1) * head)
                t = acc[:, cols]
                ms = jnp.mean(t * t, axis=-1, keepdims=True)
                o_ref[rows, cols] = ((t * lax.rsqrt(ms + EPS)) * gain_ref[:, cols]).astype(o_ref.dtype)
        elif kind == "gate":
            o_ref[rows, :] = jax.nn.sigmoid(acc + rest[0][...]).astype(o_ref.dtype)
        else:
            o_ref[rows, :] = acc.astype(o_ref.dtype)


def _proj(kind, h, w_all, layer, col0, N, head=None, vec=None, cos=None, sin=None):
    T, D = h.shape
    tm = _tile(T, 1024)
    tn = _tile(N, 1024)
    while col0 % tn:
        tn //= 2
    sub = _tile(tm, 256)
    assert tn % LANES == 0 and (head is None or tn % head == 0)
    jb = col0 // tn
    in_specs = [pl.BlockSpec((tm, D), lambda j, i: (i, 0)),
                pl.BlockSpec((1, D, tn), lambda j, i: (layer, 0, jb + j))]
    args = [h, w_all]
    if kind == "rotary":
        in_specs += [pl.BlockSpec((tm, head // 2), lambda j, i: (i, 0))] * 2
        args += [cos, sin]
    if vec is not None:
        in_specs.append(pl.BlockSpec((1, tn), lambda j, i: (0, j)))
        args.append(vec)
    return pl.pallas_call(
        functools.partial(_proj_kernel, kind, sub, head),
        out_shape=jax.ShapeDtypeStruct((T, N), BF16),
        grid=(N // tn, T // tm),
        in_specs=in_specs,
        out_specs=pl.BlockSpec((tm, tn), lambda j, i: (i, j)),
        scratch_shapes=[pltpu.VMEM((D, tn), BF16)],
        compiler_params=_params("arbitrary", "arbitrary"),
        name="inproj_" + kind,
    )(*args)


def _retention_kernel(C, G, q_ref, k_ref, v_ref, g_ref, s0_ref, intra_ref, cross_ref, sdec_ref,
                      cdec_ref, gnw_ref, *rest):
    o_ref, s_ref = rest[-2], rest[-1]
    n = pl.program_id(2)
    HP = s_ref.shape[1]
    dk = s_ref.shape[2]
    dv = s_ref.shape[3]

    @pl.when(n == 0)
    def _():
        s_ref[...] = s0_ref[...]

    def body(g, carry):
        r0 = pl.multiple_of(g * C, C)
        for hh in range(HP):
            qk_cols = slice(hh * dk, (hh + 1) * dk)
            v_cols = slice(hh * dv, (hh + 1) * dv)
            q = q_ref[pl.ds(r0, C), qk_cols]
            k = k_ref[pl.ds(r0, C), qk_cols]
            v = v_ref[pl.ds(r0, C), v_cols]
            s = s_ref[0, hh]
            scores = lax.dot_general(q, k, NT_DIMS, preferred_element_type=F32) * intra_ref[hh]
            o = jnp.dot(scores.astype(BF16), v, preferred_element_type=F32)
            o = o + jnp.dot(q, s.astype(BF16), preferred_element_type=F32) * cross_ref[hh]
            kd_t = (k.astype(F32) * sdec_ref[hh]).T.astype(BF16)
            s_ref[0, hh] = cdec_ref[hh] * s + jnp.dot(kd_t, v, preferred_element_type=F32)
            mu = jnp.mean(o, axis=-1, keepdims=True)
            d = o - mu
            var = jnp.mean(d * d, axis=-1, keepdims=True)
            y = d * lax.rsqrt(var + EPS) * gnw_ref[hh]
            gate = g_ref[pl.ds(r0, C), v_cols].astype(F32)
            o_ref[pl.ds(r0, C), v_cols] = (gate * jax.nn.sigmoid(gate) * y).astype(o_ref.dtype)
        return carry

    lax.fori_loop(0, G, body, 0)


def _decay_tables(H, C):
    log_gamma = jnp.log1p(-jnp.exp2(-5.0 - jnp.arange(H, dtype=F32)))
    idx = jnp.arange(C, dtype=F32)
    diff = idx[:, None] - idx[None, :]
    intra = jnp.where(diff >= 0, jnp.exp(log_gamma[:, None, None] * jnp.maximum(diff, 0.0)), 0.0)
    cross = jnp.exp(log_gamma[:, None] * (idx + 1.0))[:, :, None]
    sdec = jnp.exp(log_gamma[:, None] * (C - 1.0 - idx))[:, :, None]
    cdec = jnp.exp(log_gamma * C)[:, None, None]
    return intra, cross, sdec, cdec


def _retention(rqk, rvg, s0, s0_b0, gnw, out_prev, row0, B, S, C, G, dims):
    ret_qk, ret_v, att_w, dk, hd = dims
    T = rqk.shape[0]
    H = ret_qk // dk
    dv = ret_v // H
    rows = C * G
    nblk = S // rows
    assert S % rows == 0 and row0 % rows == 0
    blk0 = row0 // rows
    intra, cross, sdec, cdec = _decay_tables(H, C)
    cdec = jnp.broadcast_to(cdec, (H, 1, dv))
    vcol = 0

    HP = 2 if H % 2 == 0 else 1
    kcol, gcol = H // HP, H // HP

    def rowmap(col0):
        return lambda b, h, n: (blk0 + b * nblk + n, col0 + h)

    per_head = lambda b, h, n: (h, 0, 0)
    in_specs = [
        pl.BlockSpec((rows, HP * dk), rowmap(0)),
        pl.BlockSpec((rows, HP * dk), rowmap(kcol)),
        pl.BlockSpec((rows, HP * dv), rowmap(vcol)),
        pl.BlockSpec((rows, HP * dv), rowmap(gcol)),
        pl.BlockSpec((1, HP, dk, dv), lambda b, h, n: (s0_b0 + b, h, 0, 0)),
        pl.BlockSpec((HP, C, C), per_head),
        pl.BlockSpec((HP, C, 1), per_head),
        pl.BlockSpec((HP, C, 1), per_head),
        pl.BlockSpec((HP, 1, dv), per_head),
        pl.BlockSpec((HP, 1, dv), per_head),
    ]
    args = [rqk, rqk, rvg, rvg, s0, intra, cross, sdec, cdec, gnw.reshape(H, 1, dv)]
    aliases = {}
    if out_prev is not None:
        in_specs.append(pl.BlockSpec(memory_space=pl.ANY))
        args.append(out_prev)
        aliases = {len(args) - 1: 0}
    return pl.pallas_call(
        functools.partial(_retention_kernel, C, G),
        out_shape=(jax.ShapeDtypeStruct((T, ret_v), BF16), jax.ShapeDtypeStruct((B, H, dk, dv), F32)),
        grid=(B, H // HP, nblk),
        in_specs=in_specs,
        out_specs=(
            pl.BlockSpec((rows, HP * dv), rowmap(0)),
            pl.BlockSpec((1, HP, dk, dv), lambda b, h, n: (b, h, 0, 0)),
        ),
        input_output_aliases=aliases,
        compiler_params=_params("parallel", "parallel", "arbitrary"),
        name="retention",
    )(*args)


def _attention_kernel(C, R, NC, H, hd, mask_first, cache_layer, q_ref, kp_ref, kc_ref, vp_ref, vc_ref,
                      bias_ref, *rest):
    i = pl.program_id(1)
    SB = NC * C
    prev_cache = cache_layer is not None
    prev_off = jnp.where(i > 0, 0.0, NEG_INF) if mask_first else None
    if prev_cache:
        o_ref, kbuf, vbuf, sem = rest[-4:]
        b = pl.program_id(0)

        def head_copies(seq, slot):
            for h in range(H):
                yield pltpu.make_async_copy(kp_ref.at[cache_layer, seq, :, h, :], kbuf.at[slot, h], sem.at[slot])
                yield pltpu.make_async_copy(vp_ref.at[cache_layer, seq, :, h, :], vbuf.at[slot, h], sem.at[slot])

        @pl.when(b == 0)
        def _():
            for cp in head_copies(0, 0):
                cp.start()

        @pl.when(b + 1 < pl.num_programs(0))
        def _():
            for cp in head_copies(b + 1, (b + 1) % 2):
                cp.start()

        slot = b % 2
        for cp in head_copies(b, slot):
            cp.wait()
    else:
        o_ref = rest[-1]

    for sb in range(q_ref.shape[0] // SB):
        r0 = sb * SB
        n_prev = max(R - r0, 0)
        c0 = max(r0 - R, 0)
        for h in range(H):
            lanes = slice(h * hd, (h + 1) * hd)
            q = q_ref[r0 : r0 + SB, lanes]
            kc = kc_ref[c0 : r0 + SB, lanes]
            vc = vc_ref[c0 : r0 + SB, lanes]
            sc = lax.dot_general(q, kc, NT_DIMS, preferred_element_type=F32) + bias_ref[h, :, n_prev:]
            m = jnp.max(sc, axis=-1, keepdims=True)
            if n_prev:
                if prev_cache:
                    kp = kbuf[slot, h, r0:R, :].astype(BF16)
                    vp = vbuf[slot, h, r0:R, :].astype(BF16)
                else:
                    kp = kp_ref[r0:R, lanes]
                    vp = vp_ref[r0:R, lanes]
                sp = lax.dot_general(q, kp, NT_DIMS, preferred_element_type=F32) + bias_ref[h, :, :n_prev]
                if mask_first:
                    sp = sp + prev_off
                m = jnp.maximum(m, jnp.max(sp, axis=-1, keepdims=True))
            pc = jnp.exp(sc - m)
            l = jnp.sum(pc, axis=-1, keepdims=True)
            o = jnp.dot(pc.astype(BF16), vc, preferred_element_type=F32)
            if n_prev:
                pp = jnp.exp(sp - m)
                l = l + jnp.sum(pp, axis=-1, keepdims=True)
                o = o + jnp.dot(pp.astype(BF16), vp, preferred_element_type=F32)
            o_ref[r0 : r0 + SB, lanes] = (o * (1.0 / l)).astype(o_ref.dtype)


def _attention(aqk, av, k_past, v_past, past_layer, bias, out_prev, row0, B, S, dims):
    ret_qk, ret_v, att_w, dk, hd = dims
    T = aqk.shape[0]
    H = att_w // hd
    C = min(CHUNK, S)
    SB = bias.shape[1]
    R = bias.shape[2] - SB
    fresh = k_past is None
    rows = R if fresh else S
    assert S % rows == 0 and row0 % rows == 0 and rows % SB == 0 and SB % C == 0
    nblk = S // rows
    blk0 = row0 // rows

    def cur(col):
        return pl.BlockSpec((rows, att_w), lambda b, i: (blk0 + b * nblk + i, col))

    if fresh:
        def prev(col):
            return pl.BlockSpec((rows, att_w), lambda b, i: (blk0 + b * nblk + jnp.maximum(i - 1, 0), col))

        kp_spec, vp_spec = prev(1), prev(0)
        kp_arg, vp_arg = aqk, av
    else:
        assert nblk == 1
        kp_spec = pl.BlockSpec(memory_space=pl.ANY)
        vp_spec = kp_spec
        kp_arg, vp_arg = k_past, v_past
    in_specs = [cur(0), kp_spec, cur(1), vp_spec, cur(0),
                pl.BlockSpec((H, SB, R + SB), lambda b, i: (0, 0, 0), pipeline_mode=pl.Buffered(1))]
    args = [aqk, kp_arg, aqk, vp_arg, av, bias]
    aliases = {}
    if out_prev is not None:
        in_specs.append(pl.BlockSpec(memory_space=pl.ANY))
        args.append(out_prev)
        aliases = {len(args) - 1: 0}
    scratch = []
    if not fresh:
        scratch = [pltpu.VMEM((2, H, R, hd), F32), pltpu.VMEM((2, H, R, hd), F32), pltpu.SemaphoreType.DMA((2,))]
    return pl.pallas_call(
        functools.partial(_attention_kernel, C, R, SB // C, H, hd, fresh, None if fresh else past_layer),
        out_shape=jax.ShapeDtypeStruct((T, att_w), BF16),
        grid=(B, nblk),
        in_specs=in_specs,
        out_specs=pl.BlockSpec((rows, att_w), lambda b, i: (blk0 + b * nblk + i, 0)),
        scratch_shapes=scratch,
        input_output_aliases=aliases,
        compiler_params=_params("parallel" if fresh else "arbitrary", "arbitrary"),
        name="attention",
    )(*args)


def _merge_kernel(ret_ref, wr_ref, att_ref, wa_ref, gr_ref, ga_ref, o_ref):
    y_ret = jnp.dot(ret_ref[...], wr_ref[...], preferred_element_type=F32)
    y_att = jnp.dot(att_ref[...], wa_ref[...], preferred_element_type=F32)
    merged = gr_ref[...].astype(F32) * y_ret + ga_ref[...].astype(F32) * y_att
    o_ref[...] = merged.astype(o_ref.dtype)


def _merge(gated, att, sg, w_ret_o, w_att_o):
    T, ret_v = gated.shape
    att_w = att.shape[1]
    D = w_ret_o.shape[1]
    tm = _tile(T, 256)
    tn = _tile(D, 1024)
    gcol = 0
    return pl.pallas_call(
        _merge_kernel,
        out_shape=jax.ShapeDtypeStruct((T, D), BF16),
        grid=(D // tn, T // tm),
        in_specs=[
            pl.BlockSpec((tm, ret_v), lambda j, i: (i, 0)),
            pl.BlockSpec((ret_v, tn), lambda j, i: (0, j)),
            pl.BlockSpec((tm, att_w), lambda j, i: (i, 0)),
            pl.BlockSpec((att_w, tn), lambda j, i: (0, j)),
            pl.BlockSpec((tm, tn), lambda j, i: (i, gcol + j)),
            pl.BlockSpec((tm, tn), lambda j, i: (i, gcol + D // tn + j)),
        ],
        out_specs=pl.BlockSpec((tm, tn), lambda j, i: (i, j)),
        compiler_params=_params("parallel", "parallel"),
        name="merge",
    )(gated, w_ret_o, att, w_att_o, sg, sg)


def _outproj_kernel(NG, NE, m_ref, w_ref, x_ref, nw_ref, wr_ref, br_ref, x1_ref, h2_ref, ids_ref,
                    gates_ref):
    x1 = x_ref[...] + jnp.dot(m_ref[...], w_ref[...], preferred_element_type=F32)
    x1_ref[...] = x1
    ms = jnp.mean(x1 * x1, axis=-1, keepdims=True)
    h2 = (x1 * lax.rsqrt(ms + EPS)) * nw_ref[...]
    h2_ref[...] = h2
    h_hi = h2.astype(BF16)
    h_lo = (h2 - h_hi.astype(F32)).astype(BF16)
    r_hi = jnp.dot(h_hi, wr_ref[...], preferred_element_type=F32)
    r_lo = jnp.dot(h_lo, wr_ref[:, :LANES], preferred_element_type=F32)
    logits = (r_hi[:, :LANES] + r_hi[:, LANES:]) + r_lo + br_ref[...]
    epg = NE // NG
    lane = lax.broadcasted_iota(jnp.int32, logits.shape, 1).astype(F32)
    big = float(LANES)
    gmask = lane < NG
    gl = jnp.where(gmask, logits, NEG_INF)
    gm = jnp.max(gl, axis=-1, keepdims=True)
    gsel = jnp.min(jnp.where(gl == gm, lane, big), axis=-1, keepdims=True)
    pg = 1.0 / jnp.sum(jnp.where(gmask, jnp.exp(gl - gm), 0.0), axis=-1, keepdims=True)
    lo = NG + gsel * epg
    ingrp = (lane >= lo) & (lane < lo + epg)
    el = jnp.where(ingrp, logits, NEG_INF)
    v1 = jnp.max(el, axis=-1, keepdims=True)
    i1 = jnp.min(jnp.where(el == v1, lane, big), axis=-1, keepdims=True)
    el2 = jnp.where(lane == i1, NEG_INF, el)
    v2 = jnp.max(el2, axis=-1, keepdims=True)
    i2 = jnp.min(jnp.where(el2 == v2, lane, big), axis=-1, keepdims=True)
    t = jnp.exp(v2 - v1)
    den = 1.0 / (1.0 + t)
    g1 = pg * den
    g2 = pg * (t * den)
    ids = jnp.where(lane == 0.0, i1 - NG, jnp.where(lane == 1.0, i2 - NG, 0.0))
    ids_ref[...] = ids.astype(jnp.int32)
    gates_ref[...] = jnp.where(lane == 0.0, g1, jnp.where(lane == 1.0, g2, 0.0))


def _outproj(merged, w_out, x, nw, wr, br, NG, NE):
    T, D = x.shape
    tm = _tile(T, 256)
    row = lambda i: (i, 0)
    const = lambda i: (0, 0)
    return pl.pallas_call(
        functools.partial(_outproj_kernel, NG, NE),
        out_shape=(
            jax.ShapeDtypeStruct((T, D), F32),
            jax.ShapeDtypeStruct((T, D), F32),
            jax.ShapeDtypeStruct((T, LANES), jnp.int32),
            jax.ShapeDtypeStruct((T, LANES), F32),
        ),
        grid=(T // tm,),
        in_specs=[
            pl.BlockSpec((tm, D), row),
            pl.BlockSpec((D, D), const),
            pl.BlockSpec((tm, D), row),
            pl.BlockSpec((1, D), const),
            pl.BlockSpec((D, 2 * LANES), const),
            pl.BlockSpec((1, LANES), const),
        ],
        out_specs=(
            pl.BlockSpec((tm, D), row),
            pl.BlockSpec((tm, D), row),
            pl.BlockSpec((tm, LANES), row),
            pl.BlockSpec((tm, LANES), row),
        ),
        compiler_params=_params("parallel"),
        name="outproj_router",
    )(merged, w_out, x, nw.reshape(1, D), wr, br)


def _slot_tables(ids, NE, blk, NB):
    e_flat = ids[:, :TOP_K].reshape(-1)
    A = e_flat.shape[0]
    seg = _tile(A, 256)
    onehot = (e_flat[:, None] == jnp.arange(NE, dtype=jnp.int32)[None, :]).reshape(A // seg, seg, NE)
    tril = jnp.tril(jnp.ones((seg, seg), BF16))
    within = jnp.einsum("rc,tce->tre", tril, onehot.astype(BF16), preferred_element_type=F32)
    seg_tot = within[:, -1, :]
    seg_end = jnp.cumsum(seg_tot, axis=0)
    csum = (within + (seg_end - seg_tot)[:, None, :]).astype(jnp.int32)
    counts = seg_end[-1].astype(jnp.int32)
    oh = onehot.astype(jnp.int32)
    rank = jnp.sum(oh * csum, axis=2).reshape(A) - 1
    pcounts = (counts + blk - 1) // blk * blk
    pends = jnp.cumsum(pcounts)
    pstarts = pends - pcounts
    slots = jnp.sum(oh * pstarts[None, None, :], axis=2).reshape(A) + rank
    blk_start = jnp.arange(NB, dtype=jnp.int32) * blk
    block_e = jnp.minimum(jnp.sum((pends[None, :] <= blk_start[:, None]).astype(jnp.int32), axis=1), NE - 1)
    n_used = (pends[-1] // blk).reshape(1)
    pad_rows = jnp.where(pcounts > 0, pends - blk, -1)
    return (slots.astype(jnp.int32), block_e.astype(jnp.int32), n_used.astype(jnp.int32),
            pad_rows.astype(jnp.int32))


def _row_copy(src_ref, src_row, dst_ref, dst_row, sem):
    return pltpu.make_async_copy(src_ref.at[pl.ds(src_row, 1)], dst_ref.at[pl.ds(dst_row, 1)], sem)


def _dispatch_kernel(tm, blk, slots_ref, pad_rows_ref, h_ref, buf_ref, zeros, sem):
    base = pl.program_id(0) * (TOP_K * tm)

    @pl.when(pl.program_id(0) == 0)
    def _():
        zeros[...] = jnp.zeros_like(zeros)
        for e in range(pad_rows_ref.shape[0]):
            @pl.when(pad_rows_ref[e] >= 0)
            def _():
                row0 = pl.multiple_of(pad_rows_ref[e], blk)
                pltpu.make_async_copy(zeros, buf_ref.at[pl.ds(row0, blk)], sem).start()
        for e in range(pad_rows_ref.shape[0]):
            @pl.when(pad_rows_ref[e] >= 0)
            def _():
                pltpu.make_async_copy(zeros, buf_ref.at[pl.ds(0, blk)], sem).wait()

    def start(r, carry):
        for kk in range(TOP_K):
            _row_copy(h_ref, r, buf_ref, slots_ref[base + TOP_K * r + kk], sem).start()
        return carry

    lax.fori_loop(0, tm, start, 0, unroll=DMA_ISSUE_UNROLL)
    for kk in range(TOP_K):
        pltpu.make_async_copy(h_ref, buf_ref.at[pl.ds(0, tm)], sem).wait()


def _dispatch(h2, slots, pad_rows, P, blk):
    T, D = h2.shape
    tm = _tile(T, 512)
    grid_spec = pltpu.PrefetchScalarGridSpec(
        num_scalar_prefetch=2,
        grid=(T // tm,),
        in_specs=[pl.BlockSpec((tm, D), lambda i, s, p: (i, 0))],
        out_specs=pl.BlockSpec(memory_space=pl.ANY),
        scratch_shapes=[pltpu.VMEM((blk, D), F32), pltpu.SemaphoreType.DMA(())],
    )
    return pl.pallas_call(
        functools.partial(_dispatch_kernel, tm, blk),
        out_shape=jax.ShapeDtypeStruct((P, D), F32),
        grid_spec=grid_spec,
        compiler_params=_params("arbitrary"),
        name="moe_dispatch",
    )(slots, pad_rows, h2)


def _experts_kernel(block_e_ref, n_used_ref, x_ref, wu_ref, wd_ref, o_ref):
    i = pl.program_id(0)

    @pl.when(i < n_used_ref[0])
    def _():
        x = x_ref[...].astype(BF16)
        up = jnp.dot(x, wu_ref[0], preferred_element_type=F32)
        ff = up.shape[1] // 2
        a = up[:, :ff]
        b = up[:, ff:]
        hmid = (a * jax.nn.sigmoid(a) * b).astype(BF16)
        o_ref[...] = jnp.dot(hmid, wd_ref[0], preferred_element_type=F32)

    @pl.when(i >= n_used_ref[0])
    def _():
        o_ref[...] = jnp.zeros_like(o_ref)


def _experts(buf, block_e, n_used, w_up, w_down, blk):
    P, D = buf.shape
    NB = P // blk
    ff2 = w_up.shape[2]
    grid_spec = pltpu.PrefetchScalarGridSpec(
        num_scalar_prefetch=2,
        grid=(NB,),
        in_specs=[
            pl.BlockSpec((blk, D), lambda i, be, nu: (jnp.where(i < nu[0], i, 0), 0)),
            pl.BlockSpec((1, D, ff2), lambda i, be, nu: (be[i], 0, 0)),
            pl.BlockSpec((1, ff2 // 2, D), lambda i, be, nu: (be[i], 0, 0)),
        ],
        out_specs=pl.BlockSpec((blk, D), lambda i, be, nu: (i, 0)),
    )
    return pl.pallas_call(
        _experts_kernel,
        out_shape=jax.ShapeDtypeStruct((P, D), F32),
        grid_spec=grid_spec,
        compiler_params=_params("arbitrary"),
        name="moe_experts",
    )(block_e, n_used, buf, w_up, w_down)


def _combine_kernel(tm, with_norm, split_tiles, slots_ref, x_ref, gates_ref, y_ref, *rest):
    if with_norm:
        nw_ref, o_ref, h_ref, rows, sem = rest
    elif split_tiles is not None:
        o_ref, o2_ref, rows, sem = rest
    else:
        o_ref, rows, sem = rest
    i = pl.program_id(0)
    n = pl.num_programs(0)

    def fetch(tile, b):
        base = tile * (TOP_K * tm)

        def start(r, carry):
            for kk in range(TOP_K):
                _row_copy(y_ref, slots_ref[base + TOP_K * r + kk], rows.at[b, kk], r, sem.at[b]).start()
            return carry

        lax.fori_loop(0, tm, start, 0, unroll=DMA_ISSUE_UNROLL)

    @pl.when(i == 0)
    def _():
        fetch(0, 0)

    @pl.when(i + 1 < n)
    def _():
        fetch(i + 1, (i + 1) % 2)

    b = i % 2
    for kk in range(TOP_K):
        pltpu.make_async_copy(y_ref.at[pl.ds(0, tm)], rows.at[b, kk], sem.at[b]).wait()
    g = gates_ref[...]
    y = g[:, 0:1] * rows[b, 0]
    for kk in range(1, TOP_K):
        y = y + g[:, kk : kk + 1] * rows[b, kk]
    out = x_ref[...] + y
    if split_tiles is None:
        o_ref[...] = out
    else:
        @pl.when(i < split_tiles)
        def _():
            o_ref[...] = out

        @pl.when(i >= split_tiles)
        def _():
            o2_ref[...] = out
    if with_norm:
        ms = jnp.mean(out * out, axis=-1, keepdims=True)
        h_ref[...] = ((out * lax.rsqrt(ms + EPS)) * nw_ref[...]).astype(h_ref.dtype)


def _combine(x1, ybuf, slots, gates, next_norm_w=None, split_rows=None):
    T, D = x1.shape
    tm = _tile(T, 256)
    with_norm = next_norm_w is not None
    split_tiles = None
    row = pl.BlockSpec((tm, D), lambda i, s: (i, 0))
    in_specs = [row, pl.BlockSpec((tm, LANES), lambda i, s: (i, 0)), pl.BlockSpec(memory_space=pl.ANY)]
    args = [slots, x1, gates, ybuf]
    out_shape = jax.ShapeDtypeStruct((T, D), F32)
    out_specs = row
    if with_norm:
        in_specs.append(pl.BlockSpec((1, D), lambda i, s: (0, 0)))
        args.append(next_norm_w.reshape(1, D))
        out_shape = (out_shape, jax.ShapeDtypeStruct((T, D), BF16))
        out_specs = (row, row)
    elif split_rows is not None:
        assert split_rows % tm == 0 and 0 < split_rows < T
        split_tiles = split_rows // tm
        out_shape = (jax.ShapeDtypeStruct((split_rows, D), F32), jax.ShapeDtypeStruct((T - split_rows, D), F32))
        out_specs = (pl.BlockSpec((tm, D), lambda i, s: (jnp.minimum(i, split_tiles - 1), 0)),
                     pl.BlockSpec((tm, D), lambda i, s: (jnp.maximum(i - split_tiles, 0), 0)))
    grid_spec = pltpu.PrefetchScalarGridSpec(
        num_scalar_prefetch=1,
        grid=(T // tm,),
        in_specs=in_specs,
        out_specs=out_specs,
        scratch_shapes=[pltpu.VMEM((2, TOP_K, tm, D), F32), pltpu.SemaphoreType.DMA((2,))],
    )
    return pl.pallas_call(
        functools.partial(_combine_kernel, tm, with_norm, split_tiles),
        out_shape=out_shape,
        grid_spec=grid_spec,
        compiler_params=_params("arbitrary"),
        name="moe_combine",
    )(*args)


def _rope_tables(Bp, Sp, Bs, Ss, half):
    freqs = ROPE_BASE ** (-jnp.arange(half, dtype=F32) / half)
    pos = jnp.concatenate([jnp.tile(jnp.arange(Sp), Bp), jnp.tile(PAST_LEN + jnp.arange(Ss), Bs)])
    ang = pos.astype(F32)[:, None] * freqs[None, :]
    return jnp.cos(ang), jnp.sin(ang)


def _band_bias(rel_table, C, R, SB):
    H = rel_table.shape[0]
    max_rel = (rel_table.shape[1] - 1) // 2
    L = R + 2 * SB
    t = np.arange(L)
    diag_idx = np.clip(R + SB - 1 - t, -max_rel, max_rel) + max_rel
    diag = jnp.take(rel_table.astype(F32), jnp.asarray(diag_idx, jnp.int32), axis=1)
    flat = jnp.tile(diag, (1, SB))
    toep = flat[:, SB - 1 : SB - 1 + SB * (L - 1)].reshape(H, SB, L - 1)[:, :, : R + SB]
    qc = np.arange(SB)[:, None] // C
    kc = np.arange(R + SB)[None, :] // C
    inband = (kc >= qc) & (kc <= qc + R // C)
    return jnp.where(jnp.asarray(inband)[None], toep, NEG_INF)


def kernel(x_prompt, x_sample, state_ret, cache_att_k, cache_att_v, norm_mix_w, w_in, ret_gn_w,
           w_ret_o, q_norm_w, k_norm_w, rel_bias, w_att_o, b_gate, w_out, norm_ffn_w, w_router_g,
           b_router_g, w_router_e, b_router_e, w_up, w_down):
    Bp, Sp, D = x_prompt.shape
    Bs, Ss, _ = x_sample.shape
    depth = w_in.shape[0]
    _, _, H_ret, dk, dv = state_ret.shape
    _, _, R, H_att, hd = cache_att_k.shape
    ret_qk, ret_v, att_w = H_ret * dk, H_ret * dv, H_att * hd
    dims = (ret_qk, ret_v, att_w, dk, hd)
    NG, NE = w_router_g.shape[2], w_router_e.shape[2]
    Tp, Ts = Bp * Sp, Bs * Ss
    T = Tp + Ts
    keep = min(R, Sp)
    assert NG + NE <= LANES and R % CHUNK == 0

    blk = 256
    NB = (T * TOP_K + NE * (blk - 1) + blk - 1) // blk
    C_p = 256 if Sp % 256 == 0 else min(CHUNK, Sp)
    G_p = max(1, min(1024, Sp) // C_p)
    C_s = min(CHUNK, Ss)

    x, h = _concat_rmsnorm(x_prompt.reshape(Tp, D), x_sample.reshape(Ts, D), norm_mix_w[0])
    cos, sin = _rope_tables(Bp, Sp, Bs, Ss, dk // 2)
    zero_state = jnp.zeros((Bp, H_ret, dk, dv), F32)
    state_all = state_ret.reshape(depth * Bs, H_ret, dk, dv)
    C_att = min(CHUNK, Sp)
    SB_p = 4 * C_att if R % (4 * C_att) == 0 else C_att
    c1 = 2 * ret_qk
    c2 = c1 + 2 * ret_v
    c3 = c2 + 2 * att_w
    c4 = c3 + att_w
    rot_scale = jnp.concatenate([jnp.ones((1, ret_qk), F32), jnp.full((1, ret_qk), float(dk) ** -0.5, F32)], axis=1)

    def mixer_input(l, h):
        qk_gain = jnp.concatenate([jnp.tile(q_norm_w[l] * (float(hd) ** -0.5), H_att),
                                   jnp.tile(k_norm_w[l], H_att)]).reshape(1, 2 * att_w)
        rqk = _proj("rotary", h, w_in, l, 0, c1, head=dk, vec=rot_scale, cos=cos, sin=sin)
        rvg = _proj("copy", h, w_in, l, c1, c2 - c1)
        aqk = _proj("norm", h, w_in, l, c2, c3 - c2, head=hd, vec=qk_gain)
        av = _proj("copy", h, w_in, l, c3, c4 - c3)
        sg = _proj("gate", h, w_in, l, c4, 2 * D, vec=b_gate[l].reshape(1, 2 * D))
        return rqk, rvg, aqk, av, sg

    rs_p, rs_s, k_p, v_p, k_s, v_s = [], [], [], [], [], []
    for l in range(depth):
        wr = jnp.zeros((D, LANES), F32).at[:, :NG].set(w_router_g[l]).at[:, NG : NG + NE].set(w_router_e[l])
        wr_hi = wr.astype(BF16)
        wr_split = jnp.concatenate([wr_hi, (wr - wr_hi.astype(F32)).astype(BF16)], axis=1)
        br = jnp.zeros((1, LANES), F32).at[0, :NG].set(b_router_g[l]).at[0, NG : NG + NE].set(b_router_e[l])
        bias_p = _band_bias(rel_bias[l], C_att, R, SB_p)
        bias_s = _band_bias(rel_bias[l], min(CHUNK, Ss), R, min(CHUNK, Ss))

        rqk, rvg, aqk, av, sg = mixer_input(l, h)

        gated, st_p = _retention(rqk, rvg, zero_state, 0, ret_gn_w[l], None, 0, Bp, Sp, C_p, G_p, dims)
        gated, st_s = _retention(rqk, rvg, state_all, l * Bs, ret_gn_w[l], gated, Tp, Bs, Ss, C_s, 1, dims)
        att = _attention(aqk, av, None, None, 0, bias_p, None, 0, Bp, Sp, dims)
        att = _attention(aqk, av, cache_att_k, cache_att_v, l, bias_s, att, Tp, Bs, Ss, dims)

        merged = _merge(gated, att, sg, w_ret_o[l].astype(BF16), w_att_o[l].astype(BF16))
        x1, h2, ids, gates = _outproj(merged, w_out[l].astype(BF16), x, norm_ffn_w[l], wr_split, br, NG, NE)

        slots, block_e, n_used, pad_rows = _slot_tables(ids, NE, blk, NB)
        buf = _dispatch(h2, slots, pad_rows, NB * blk, blk)
        ybuf = _experts(buf, block_e, n_used, w_up[l].astype(BF16), w_down[l].astype(BF16), blk)
        if l + 1 < depth:
            x, h = _combine(x1, ybuf, slots, gates, next_norm_w=norm_mix_w[l + 1])
        else:
            y_p, y_s = _combine(x1, ybuf, slots, gates, split_rows=Tp)

        rs_p.append(st_p)
        rs_s.append(st_s)

        def newest(a, lo, hi):
            rows_p = jnp.stack([lax.slice(a, ((b + 1) * Sp - keep, lo), ((b + 1) * Sp, hi)) for b in range(Bp)])
            rows_s = lax.slice(a, (Tp, lo), (T, hi))
            return (rows_p.astype(F32).reshape(Bp, keep, H_att, hd), rows_s.astype(F32).reshape(Bs, Ss, H_att, hd))

        kp_l, ks_l = newest(aqk, att_w, 2 * att_w)
        vp_l, vs_l = newest(av, 0, att_w)
        k_p.append(kp_l)
        v_p.append(vp_l)
        k_s.append(ks_l)
        v_s.append(vs_l)

    return (y_p.reshape(Bp, Sp, D), y_s.reshape(Bs, Ss, D), jnp.stack(rs_p), jnp.stack(rs_s), jnp.stack(k_p), jnp.stack(v_p),
            jnp.stack(k_s), jnp.stack(v_s))
```

```python
import functools

import jax
import jax.numpy as jnp
import numpy as np
from jax import lax
from jax.experimental import pallas as pl
from jax.experimental.pallas import tpu as pltpu

F32 = jnp.float32
BF16 = jnp.bfloat16

EPS = 1e-6
NEG_INF = -1e30
LOG2_E = 1.4426950408889634
ROPE_BASE = 10000.0
PAST_LEN = 1024
CHUNK = 64
TOP_K = 2

V7X_VMEM_LIMIT_BYTES = 56 * 1024 * 1024
LANES = 128
DMA_ISSUE_UNROLL = 8
NT_DIMS = (((1,), (1,)), ((), ()))


def _params(*sem):
    return pltpu.CompilerParams(dimension_semantics=sem, vmem_limit_bytes=V7X_VMEM_LIMIT_BYTES)


def _tile(n, pref):
    t = min(n, pref)
    while n % t:
        t //= 2
    assert t >= 8, (n, pref)
    return t


def _concat_rmsnorm_kernel(split_tiles, xa_ref, xb_ref, w_ref, x_ref, h_ref):
    def emit(src_ref):
        x = src_ref[...]
        x_ref[...] = x
        ms = jnp.mean(x * x, axis=-1, keepdims=True)
        h_ref[...] = ((x * lax.rsqrt(ms + EPS)) * w_ref[...]).astype(h_ref.dtype)

    @pl.when(pl.program_id(0) < split_tiles)
    def _():
        emit(xa_ref)

    @pl.when(pl.program_id(0) >= split_tiles)
    def _():
        emit(xb_ref)


def _concat_rmsnorm(xa, xb, w):
    Ta, D = xa.shape
    Tb = xb.shape[0]
    tm = _tile(Ta, 512)
    while Tb % tm:
        tm //= 2
    na = Ta // tm
    row = pl.BlockSpec((tm, D), lambda i: (i, 0))
    return pl.pallas_call(
        functools.partial(_concat_rmsnorm_kernel, na),
        out_shape=(jax.ShapeDtypeStruct((Ta + Tb, D), F32), jax.ShapeDtypeStruct((Ta + Tb, D), BF16)),
        grid=((Ta + Tb) // tm,),
        in_specs=[
            pl.BlockSpec((tm, D), lambda i: (jnp.minimum(i, na - 1), 0)),
            pl.BlockSpec((tm, D), lambda i: (jnp.maximum(i - na, 0), 0)),
            pl.BlockSpec((1, D), lambda i: (0, 0)),
        ],
        out_specs=(row, row),
        compiler_params=_params("arbitrary"),
        name="concat_rmsnorm",
    )(xa, xb, w.reshape(1, D))


def _proj_kernel(kind, sub, head, h_ref, w_ref, *rest):
    o_ref, w_bf = rest[-2], rest[-1]
    rest = rest[:-1]
    tm, tn = o_ref.shape

    @pl.when(pl.program_id(1) == 0)
    def _():
        w_bf[...] = w_ref[0].astype(BF16)

    for k in range(tm // sub):
        rows = slice(k * sub, (k + 1) * sub)
        acc = jnp.dot(h_ref[rows, :], w_bf[...], preferred_element_type=F32)
        if kind == "rotary":
            cos_ref, sin_ref, scale_ref = rest[:3]
            cos = cos_ref[rows, :]
            sin = sin_ref[rows, :]
            half = head // 2
            for hh in range(tn // head):
                lo = slice(hh * head, hh * head + half)
                hi = slice(hh * head + half, (hh + 1) * head)
                t1 = acc[:, lo]
                t2 = acc[:, hi]
                o_ref[rows, lo] = ((t1 * cos - t2 * sin) * scale_ref[:, lo]).astype(o_ref.dtype)
                o_ref[rows, hi] = ((t1 * sin + t2 * cos) * scale_ref[:, hi]).astype(o_ref.dtype)
        elif kind == "norm":
            gain_ref = rest[0]
            for hh in range(tn // head):
                cols = slice(hh * head, (hh + 1) * head)
                t = acc[:, cols]
                ms = jnp.mean(t * t, axis=-1, keepdims=True)
                o_ref[rows, cols] = ((t * lax.rsqrt(ms + EPS)) * gain_ref[:, cols]).astype(o_ref.dtype)
        elif kind == "gate":
            o_ref[rows, :] = jax.nn.sigmoid(acc + rest[0][...]).astype(o_ref.dtype)
        else:
            o_ref[rows, :] = acc.astype(o_ref.dtype)


def _proj(kind, h, w_all, layer, col0, N, head=None, vec=None, cos=None, sin=None):
    T, D = h.shape
    tm = _tile(T, 1024)
    tn = _tile(N, 1024)
    while col0 % tn:
        tn //= 2
    sub = _tile(tm, 256)
    assert tn % LANES == 0 and (head is None or tn % head == 0)
    jb = col0 // tn
    in_specs = [pl.BlockSpec((tm, D), lambda j, i: (i, 0)),
                pl.BlockSpec((1, D, tn), lambda j, i: (layer, 0, jb + j))]
    args = [h, w_all]
    if kind == "rotary":
        in_specs += [pl.BlockSpec((tm, head // 2), lambda j, i: (i, 0))] * 2
        args += [cos, sin]
    if vec is not None:
        in_specs.append(pl.BlockSpec((1, tn), lambda j, i: (0, j)))
        args.append(vec)
    return pl.pallas_call(
        functools.partial(_proj_kernel, kind, sub, head),
        out_shape=jax.ShapeDtypeStruct((T, N), BF16),
        grid=(N // tn, T // tm),
        in_specs=in_specs,
        out_specs=pl.BlockSpec((tm, tn), lambda j, i: (i, j)),
        scratch_shapes=[pltpu.VMEM((D, tn), BF16)],
        compiler_params=_params("arbitrary", "arbitrary"),
        name="inproj_" + kind,
    )(*args)


def _retention_kernel(C, G, q_ref, k_ref, v_ref, g_ref, s0_ref, intra_ref, cross_ref, sdec_ref,
                      cdec_ref, gnw_ref, *rest):
    o_ref, s_ref = rest[-2], rest[-1]
    n = pl.program_id(2)
    HP = s_ref.shape[1]
    dk = s_ref.shape[2]
    dv = s_ref.shape[3]

    @pl.when(n == 0)
    def _():
        s_ref[...] = s0_ref[...]

    def body(g, carry):
        r0 = pl.multiple_of(g * C, C)
        for hh in range(HP):
            qk_cols = slice(hh * dk, (hh + 1) * dk)
            v_cols = slice(hh * dv, (hh + 1) * dv)
            q = q_ref[pl.ds(r0, C), qk_cols]
            k = k_ref[pl.ds(r0, C), qk_cols]
            v = v_ref[pl.ds(r0, C), v_cols]
            s = s_ref[0, hh]
            scores = lax.dot_general(q, k, NT_DIMS, preferred_element_type=F32) * intra_ref[hh]
            o = jnp.dot(scores.astype(BF16), v, preferred_element_type=F32)
            o = o + jnp.dot(q, s.astype(BF16), preferred_element_type=F32) * cross_ref[hh]
            kd_t = (k.astype(F32) * sdec_ref[hh]).T.astype(BF16)
            s_ref[0, hh] = cdec_ref[hh] * s + jnp.dot(kd_t, v, preferred_element_type=F32)
            mu = jnp.mean(o, axis=-1, keepdims=True)
            d = o - mu
            var = jnp.mean(d * d, axis=-1, keepdims=True)
            y = d * lax.rsqrt(var + EPS) * gnw_ref[hh]
            gate = g_ref[pl.ds(r0, C), v_cols].astype(F32)
            o_ref[pl.ds(r0, C), v_cols] = (gate * jax.nn.sigmoid(gate) * y).astype(o_ref.dtype)
        return carry

    lax.fori_loop(0, G, body, 0)


def _decay_tables(H, C):
    log_gamma = jnp.log1p(-jnp.exp2(-5.0 - jnp.arange(H, dtype=F32)))
    idx = jnp.arange(C, dtype=F32)
    diff = idx[:, None] - idx[None, :]
    intra = jnp.where(diff >= 0, jnp.exp(log_gamma[:, None, None] * jnp.maximum(diff, 0.0)), 0.0)
    cross = jnp.exp(log_gamma[:, None] * (idx + 1.0))[:, :, None]
    sdec = jnp.exp(log_gamma[:, None] * (C - 1.0 - idx))[:, :, None]
    cdec = jnp.exp(log_gamma * C)[:, None, None]
    return intra, cross, sdec, cdec


def _retention(rqk, rvg, s0, s0_b0, gnw, out_prev, row0, B, S, C, G, dims, state_prev=None, state_b0=0,
               state_total=None):
    ret_qk, ret_v, att_w, dk, hd = dims
    state_total = B if state_total is None else state_total
    T = rqk.shape[0]
    H = ret_qk // dk
    dv = ret_v // H
    rows = C * G
    nblk = S // rows
    assert S % rows == 0 and row0 % rows == 0
    blk0 = row0 // rows
    intra, cross, sdec, cdec = _decay_tables(H, C)
    cdec = jnp.broadcast_to(cdec, (H, 1, dv))
    vcol = 0

    HP = 2 if H % 2 == 0 else 1
    kcol, gcol = H // HP, H // HP

    def rowmap(col0):
        return lambda b, h, n: (blk0 + b * nblk + n, col0 + h)

    per_head = lambda b, h, n: (h, 0, 0)
    in_specs = [
        pl.BlockSpec((rows, HP * dk), rowmap(0)),
        pl.BlockSpec((rows, HP * dk), rowmap(kcol)),
        pl.BlockSpec((rows, HP * dv), rowmap(vcol)),
        pl.BlockSpec((rows, HP * dv), rowmap(gcol)),
        pl.BlockSpec((1, HP, dk, dv), lambda b, h, n: (s0_b0 + b, h, 0, 0)),
        pl.BlockSpec((HP, C, C), per_head),
        pl.BlockSpec((HP, C, 1), per_head),
        pl.BlockSpec((HP, C, 1), per_head),
        pl.BlockSpec((HP, 1, dv), per_head),
        pl.BlockSpec((HP, 1, dv), per_head),
    ]
    args = [rqk, rqk, rvg, rvg, s0, intra, cross, sdec, cdec, gnw.reshape(H, 1, dv)]
    aliases = {}
    if out_prev is not None:
        in_specs.append(pl.BlockSpec(memory_space=pl.ANY))
        args.append(out_prev)
        aliases[len(args) - 1] = 0
    if state_prev is not None:
        in_specs.append(pl.BlockSpec(memory_space=pl.ANY))
        args.append(state_prev)
        aliases[len(args) - 1] = 1
    return pl.pallas_call(
        functools.partial(_retention_kernel, C, G),
        out_shape=(jax.ShapeDtypeStruct((T, ret_v), BF16),
                   jax.ShapeDtypeStruct((state_total, H, dk, dv), F32)),
        grid=(B, H // HP, nblk),
        in_specs=in_specs,
        out_specs=(
            pl.BlockSpec((rows, HP * dv), rowmap(0)),
            pl.BlockSpec((1, HP, dk, dv), lambda b, h, n: (state_b0 + b, h, 0, 0)),
        ),
        input_output_aliases=aliases,
        compiler_params=_params("parallel", "parallel", "arbitrary"),
        name="retention",
    )(*args)


def _attention_kernel(C, R, NC, H, hd, mask_first, cache_layer, q_ref, kp_ref, kc_ref, vp_ref, vc_ref,
                      bias_ref, *rest):
    i = pl.program_id(1)
    SB = NC * C
    prev_cache = cache_layer is not None
    prev_off = jnp.where(i > 0, 0.0, NEG_INF) if mask_first else None
    if prev_cache:
        o_ref, kbuf, vbuf, sem = rest[-4:]
        b = pl.program_id(0)

        def head_copies(seq, slot):
            for h in range(H):
                yield pltpu.make_async_copy(kp_ref.at[cache_layer, seq, :, h, :], kbuf.at[slot, h], sem.at[slot])
                yield pltpu.make_async_copy(vp_ref.at[cache_layer, seq, :, h, :], vbuf.at[slot, h], sem.at[slot])

        @pl.when(b == 0)
        def _():
            for cp in head_copies(0, 0):
                cp.start()

        @pl.when(b + 1 < pl.num_programs(0))
        def _():
            for cp in head_copies(b + 1, (b + 1) % 2):
                cp.start()

        slot = b % 2
        for cp in head_copies(b, slot):
            cp.wait()
    else:
        o_ref = rest[-1]

    for sb in range(q_ref.shape[0] // SB):
        r0 = sb * SB
        n_prev = max(R - r0, 0)
        c0 = max(r0 - R, 0)
        for h in range(H):
            lanes = slice(h * hd, (h + 1) * hd)
            q = q_ref[r0 : r0 + SB, lanes]
            kc = kc_ref[c0 : r0 + SB, lanes]
            vc = vc_ref[c0 : r0 + SB, lanes]
            sc = lax.dot_general(q, kc, NT_DIMS, preferred_element_type=F32) + bias_ref[h, :, n_prev:]
            m = jnp.max(sc, axis=-1, keepdims=True)
            if n_prev:
                if prev_cache:
                    kp = kbuf[slot, h, r0:R, :].astype(BF16)
                    vp = vbuf[slot, h, r0:R, :].astype(BF16)
                else:
                    kp = kp_ref[r0:R, lanes]
                    vp = vp_ref[r0:R, lanes]
                sp = lax.dot_general(q, kp, NT_DIMS, preferred_element_type=F32) + bias_ref[h, :, :n_prev]
                if mask_first:
                    sp = sp + prev_off
                m = jnp.maximum(m, jnp.max(sp, axis=-1, keepdims=True))
            pc = jnp.exp2(sc - m)
            l = jnp.sum(pc, axis=-1, keepdims=True)
            o = jnp.dot(pc.astype(BF16), vc, preferred_element_type=F32)
            if n_prev:
                pp = jnp.exp2(sp - m)
                l = l + jnp.sum(pp, axis=-1, keepdims=True)
                o = o + jnp.dot(pp.astype(BF16), vp, preferred_element_type=F32)
            o_ref[r0 : r0 + SB, lanes] = (o * (1.0 / l)).astype(o_ref.dtype)


def _attention(aqk, av, k_past, v_past, past_layer, bias, out_prev, row0, B, S, dims):
    ret_qk, ret_v, att_w, dk, hd = dims
    T = aqk.shape[0]
    H = att_w // hd
    C = min(CHUNK, S)
    SB = bias.shape[1]
    R = bias.shape[2] - SB
    fresh = k_past is None
    rows = R if fresh else S
    assert S % rows == 0 and row0 % rows == 0 and rows % SB == 0 and SB % C == 0
    nblk = S // rows
    blk0 = row0 // rows

    def cur(col):
        return pl.BlockSpec((rows, att_w), lambda b, i: (blk0 + b * nblk + i, col))

    if fresh:
        def prev(col):
            return pl.BlockSpec((rows, att_w), lambda b, i: (blk0 + b * nblk + jnp.maximum(i - 1, 0), col))

        kp_spec, vp_spec = prev(1), prev(0)
        kp_arg, vp_arg = aqk, av
    else:
        assert nblk == 1
        kp_spec = pl.BlockSpec(memory_space=pl.ANY)
        vp_spec = kp_spec
        kp_arg, vp_arg = k_past, v_past
    in_specs = [cur(0), kp_spec, cur(1), vp_spec, cur(0),
                pl.BlockSpec((H, SB, R + SB), lambda b, i: (0, 0, 0), pipeline_mode=pl.Buffered(1))]
    args = [aqk, kp_arg, aqk, vp_arg, av, bias]
    aliases = {}
    if out_prev is not None:
        in_specs.append(pl.BlockSpec(memory_space=pl.ANY))
        args.append(out_prev)
        aliases = {len(args) - 1: 0}
    scratch = []
    if not fresh:
        scratch = [pltpu.VMEM((2, H, R, hd), F32), pltpu.VMEM((2, H, R, hd), F32), pltpu.SemaphoreType.DMA((2,))]
    return pl.pallas_call(
        functools.partial(_attention_kernel, C, R, SB // C, H, hd, fresh, None if fresh else past_layer),
        out_shape=jax.ShapeDtypeStruct((T, att_w), BF16),
        grid=(B, nblk),
        in_specs=in_specs,
        out_specs=pl.BlockSpec((rows, att_w), lambda b, i: (blk0 + b * nblk + i, 0)),
        scratch_shapes=scratch,
        input_output_aliases=aliases,
        compiler_params=_params("parallel" if fresh else "arbitrary", "arbitrary"),
        name="attention",
    )(*args)


def _merge_kernel(ret_ref, wr_ref, att_ref, wa_ref, gr_ref, ga_ref, o_ref):
    y_ret = jnp.dot(ret_ref[...], wr_ref[0], preferred_element_type=F32)
    y_att = jnp.dot(att_ref[...], wa_ref[0], preferred_element_type=F32)
    merged = gr_ref[...].astype(F32) * y_ret + ga_ref[...].astype(F32) * y_att
    o_ref[...] = merged.astype(o_ref.dtype)


def _merge(gated, att, sg, w_ret_o, w_att_o, layer):
    T, ret_v = gated.shape
    att_w = att.shape[1]
    D = w_ret_o.shape[2]
    tm = _tile(T, 256)
    tn = _tile(D, 1024)
    gcol = 0
    return pl.pallas_call(
        _merge_kernel,
        out_shape=jax.ShapeDtypeStruct((T, D), BF16),
        grid=(D // tn, T // tm),
        in_specs=[
            pl.BlockSpec((tm, ret_v), lambda j, i: (i, 0)),
            pl.BlockSpec((1, ret_v, tn), lambda j, i: (layer, 0, j)),
            pl.BlockSpec((tm, att_w), lambda j, i: (i, 0)),
            pl.BlockSpec((1, att_w, tn), lambda j, i: (layer, 0, j)),
            pl.BlockSpec((tm, tn), lambda j, i: (i, gcol + j)),
            pl.BlockSpec((tm, tn), lambda j, i: (i, gcol + D // tn + j)),
        ],
        out_specs=pl.BlockSpec((tm, tn), lambda j, i: (i, j)),
        compiler_params=_params("parallel", "parallel"),
        name="merge",
    )(gated, w_ret_o, att, w_att_o, sg, sg)


def _outproj_kernel(NG, NE, m_ref, w_ref, x_ref, nw_ref, wr_ref, br_ref, x1_ref, h2_ref, ids_ref,
                    gates_ref):
    x1 = x_ref[...] + jnp.dot(m_ref[...], w_ref[0], preferred_element_type=F32)
    x1_ref[...] = x1
    ms = jnp.mean(x1 * x1, axis=-1, keepdims=True)
    h2 = (x1 * lax.rsqrt(ms + EPS)) * nw_ref[...]
    h2_ref[...] = h2
    h_hi = h2.astype(BF16)
    h_lo = (h2 - h_hi.astype(F32)).astype(BF16)
    r_hi = jnp.dot(h_hi, wr_ref[...], preferred_element_type=F32)
    r_lo = jnp.dot(h_lo, wr_ref[:, :LANES], preferred_element_type=F32)
    logits = (r_hi[:, :LANES] + r_hi[:, LANES:]) + r_lo + br_ref[...]
    epg = NE // NG
    lane = lax.broadcasted_iota(jnp.int32, logits.shape, 1).astype(F32)
    big = float(LANES)
    gmask = lane < NG
    gl = jnp.where(gmask, logits, NEG_INF)
    gm = jnp.max(gl, axis=-1, keepdims=True)
    gsel = jnp.min(jnp.where(gl == gm, lane, big), axis=-1, keepdims=True)
    pg = 1.0 / jnp.sum(jnp.where(gmask, jnp.exp(gl - gm), 0.0), axis=-1, keepdims=True)
    lo = NG + gsel * epg
    ingrp = (lane >= lo) & (lane < lo + epg)
    el = jnp.where(ingrp, logits, NEG_INF)
    v1 = jnp.max(el, axis=-1, keepdims=True)
    i1 = jnp.min(jnp.where(el == v1, lane, big), axis=-1, keepdims=True)
    el2 = jnp.where(lane == i1, NEG_INF, el)
    v2 = jnp.max(el2, axis=-1, keepdims=True)
    i2 = jnp.min(jnp.where(el2 == v2, lane, big), axis=-1, keepdims=True)
    t = jnp.exp(v2 - v1)
    den = 1.0 / (1.0 + t)
    g1 = pg * den
    g2 = pg * (t * den)
    ids = jnp.where(lane == 0.0, i1 - NG, jnp.where(lane == 1.0, i2 - NG, 0.0))
    ids_ref[...] = ids.astype(jnp.int32)
    gates_ref[...] = jnp.where(lane == 0.0, g1, jnp.where(lane == 1.0, g2, 0.0))


def _outproj(merged, w_out, layer, x, nw, wr, br, NG, NE):
    T, D = x.shape
    tm = _tile(T, 256)
    row = lambda i: (i, 0)
    const = lambda i: (0, 0)
    return pl.pallas_call(
        functools.partial(_outproj_kernel, NG, NE),
        out_shape=(
            jax.ShapeDtypeStruct((T, D), F32),
            jax.ShapeDtypeStruct((T, D), F32),
            jax.ShapeDtypeStruct((T, LANES), jnp.int32),
            jax.ShapeDtypeStruct((T, LANES), F32),
        ),
        grid=(T // tm,),
        in_specs=[
            pl.BlockSpec((tm, D), row),
            pl.BlockSpec((1, D, D), lambda i: (layer, 0, 0)),
            pl.BlockSpec((tm, D), row),
            pl.BlockSpec((1, D), const),
            pl.BlockSpec((D, 2 * LANES), const),
            pl.BlockSpec((1, LANES), const),
        ],
        out_specs=(
            pl.BlockSpec((tm, D), row),
            pl.BlockSpec((tm, D), row),
            pl.BlockSpec((tm, LANES), row),
            pl.BlockSpec((tm, LANES), row),
        ),
        compiler_params=_params("parallel"),
        name="outproj_router",
    )(merged, w_out, x, nw.reshape(1, D), wr, br)


def _slot_tables(ids, NE, blk, NB):
    e_flat = ids[:, :TOP_K].reshape(-1)
    A = e_flat.shape[0]
    seg = _tile(A, 256)
    onehot = (e_flat[:, None] == jnp.arange(NE, dtype=jnp.int32)[None, :]).reshape(A // seg, seg, NE)
    tril = jnp.tril(jnp.ones((seg, seg), BF16))
    within = jnp.einsum("rc,tce->tre", tril, onehot.astype(BF16), preferred_element_type=F32)
    seg_tot = within[:, -1, :]
    seg_end = jnp.cumsum(seg_tot, axis=0)
    csum = (within + (seg_end - seg_tot)[:, None, :]).astype(jnp.int32)
    counts = seg_end[-1].astype(jnp.int32)
    oh = onehot.astype(jnp.int32)
    rank = jnp.sum(oh * csum, axis=2).reshape(A) - 1
    pcounts = (counts + blk - 1) // blk * blk
    pends = jnp.cumsum(pcounts)
    pstarts = pends - pcounts
    slots = jnp.sum(oh * pstarts[None, None, :], axis=2).reshape(A) + rank
    blk_start = jnp.arange(NB, dtype=jnp.int32) * blk
    block_e = jnp.minimum(jnp.sum((pends[None, :] <= blk_start[:, None]).astype(jnp.int32), axis=1), NE - 1)
    n_used = (pends[-1] // blk).reshape(1)
    pad_rows = jnp.where(pcounts > 0, pends - blk, -1)
    return (slots.astype(jnp.int32), block_e.astype(jnp.int32), n_used.astype(jnp.int32),
            pad_rows.astype(jnp.int32))


def _row_copy(src_ref, src_row, dst_ref, dst_row, sem):
    return pltpu.make_async_copy(src_ref.at[pl.ds(src_row, 1)], dst_ref.at[pl.ds(dst_row, 1)], sem)


def _dispatch_kernel(tm, blk, slots_ref, pad_rows_ref, h_ref, buf_ref, zeros, sem):
    base = pl.program_id(0) * (TOP_K * tm)

    @pl.when(pl.program_id(0) == 0)
    def _():
        zeros[...] = jnp.zeros_like(zeros)
        for e in range(pad_rows_ref.shape[0]):
            @pl.when(pad_rows_ref[e] >= 0)
            def _():
                row0 = pl.multiple_of(pad_rows_ref[e], blk)
                pltpu.make_async_copy(zeros, buf_ref.at[pl.ds(row0, blk)], sem).start()
        for e in range(pad_rows_ref.shape[0]):
            @pl.when(pad_rows_ref[e] >= 0)
            def _():
                pltpu.make_async_copy(zeros, buf_ref.at[pl.ds(0, blk)], sem).wait()

    def start(r, carry):
        for kk in range(TOP_K):
            _row_copy(h_ref, r, buf_ref, slots_ref[base + TOP_K * r + kk], sem).start()
        return carry

    lax.fori_loop(0, tm, start, 0, unroll=DMA_ISSUE_UNROLL)
    for kk in range(TOP_K):
        pltpu.make_async_copy(h_ref, buf_ref.at[pl.ds(0, tm)], sem).wait()


def _dispatch(h2, slots, pad_rows, P, blk):
    T, D = h2.shape
    tm = _tile(T, 512)
    grid_spec = pltpu.PrefetchScalarGridSpec(
        num_scalar_prefetch=2,
        grid=(T // tm,),
        in_specs=[pl.BlockSpec((tm, D), lambda i, s, p: (i, 0))],
        out_specs=pl.BlockSpec(memory_space=pl.ANY),
        scratch_shapes=[pltpu.VMEM((blk, D), F32), pltpu.SemaphoreType.DMA(())],
    )
    return pl.pallas_call(
        functools.partial(_dispatch_kernel, tm, blk),
        out_shape=jax.ShapeDtypeStruct((P, D), F32),
        grid_spec=grid_spec,
        compiler_params=_params("arbitrary"),
        name="moe_dispatch",
    )(slots, pad_rows, h2)


def _experts_kernel(block_e_ref, n_used_ref, x_ref, wu_ref, wd_ref, o_ref):
    i = pl.program_id(0)

    @pl.when(i < n_used_ref[0])
    def _():
        x = x_ref[...].astype(BF16)
        up = jnp.dot(x, wu_ref[0, 0], preferred_element_type=F32)
        ff = up.shape[1] // 2
        a = up[:, :ff]
        b = up[:, ff:]
        hmid = (a * jax.nn.sigmoid(a) * b).astype(BF16)
        o_ref[...] = jnp.dot(hmid, wd_ref[0, 0], preferred_element_type=F32)

    @pl.when(i >= n_used_ref[0])
    def _():
        o_ref[...] = jnp.zeros_like(o_ref)


def _experts(buf, block_e, n_used, w_up, w_down, layer, blk):
    P, D = buf.shape
    NB = P // blk
    ff2 = w_up.shape[3]
    grid_spec = pltpu.PrefetchScalarGridSpec(
        num_scalar_prefetch=2,
        grid=(NB,),
        in_specs=[
            pl.BlockSpec((blk, D), lambda i, be, nu: (jnp.where(i < nu[0], i, 0), 0)),
            pl.BlockSpec((1, 1, D, ff2), lambda i, be, nu: (layer, be[i], 0, 0)),
            pl.BlockSpec((1, 1, ff2 // 2, D), lambda i, be, nu: (layer, be[i], 0, 0)),
        ],
        out_specs=pl.BlockSpec((blk, D), lambda i, be, nu: (i, 0)),
    )
    return pl.pallas_call(
        _experts_kernel,
        out_shape=jax.ShapeDtypeStruct((P, D), F32),
        grid_spec=grid_spec,
        compiler_params=_params("arbitrary"),
        name="moe_experts",
    )(block_e, n_used, buf, w_up, w_down)


def _combine_kernel(tm, with_norm, split_tiles, slots_ref, x_ref, gates_ref, y_ref, *rest):
    if with_norm:
        nw_ref, o_ref, h_ref, rows, sem = rest
    elif split_tiles is not None:
        o_ref, o2_ref, rows, sem = rest
    else:
        o_ref, rows, sem = rest
    i = pl.program_id(0)
    n = pl.num_programs(0)

    def fetch(tile, b):
        base = tile * (TOP_K * tm)

        def start(r, carry):
            for kk in range(TOP_K):
                _row_copy(y_ref, slots_ref[base + TOP_K * r + kk], rows.at[b, kk], r, sem.at[b]).start()
            return carry

        lax.fori_loop(0, tm, start, 0, unroll=DMA_ISSUE_UNROLL)

    @pl.when(i == 0)
    def _():
        fetch(0, 0)

    @pl.when(i + 1 < n)
    def _():
        fetch(i + 1, (i + 1) % 2)

    b = i % 2
    for kk in range(TOP_K):
        pltpu.make_async_copy(y_ref.at[pl.ds(0, tm)], rows.at[b, kk], sem.at[b]).wait()
    g = gates_ref[...]
    y = g[:, 0:1] * rows[b, 0]
    for kk in range(1, TOP_K):
        y = y + g[:, kk : kk + 1] * rows[b, kk]
    out = x_ref[...] + y
    if split_tiles is None:
        o_ref[...] = out
    else:
        @pl.when(i < split_tiles)
        def _():
            o_ref[...] = out

        @pl.when(i >= split_tiles)
        def _():
            o2_ref[...] = out
    if with_norm:
        ms = jnp.mean(out * out, axis=-1, keepdims=True)
        h_ref[...] = ((out * lax.rsqrt(ms + EPS)) * nw_ref[...]).astype(h_ref.dtype)


def _combine(x1, ybuf, slots, gates, next_norm_w=None, split_rows=None):
    T, D = x1.shape
    tm = _tile(T, 256)
    with_norm = next_norm_w is not None
    split_tiles = None
    row = pl.BlockSpec((tm, D), lambda i, s: (i, 0))
    in_specs = [row, pl.BlockSpec((tm, LANES), lambda i, s: (i, 0)), pl.BlockSpec(memory_space=pl.ANY)]
    args = [slots, x1, gates, ybuf]
    out_shape = jax.ShapeDtypeStruct((T, D), F32)
    out_specs = row
    if with_norm:
        in_specs.append(pl.BlockSpec((1, D), lambda i, s: (0, 0)))
        args.append(next_norm_w.reshape(1, D))
        out_shape = (out_shape, jax.ShapeDtypeStruct((T, D), BF16))
        out_specs = (row, row)
    elif split_rows is not None:
        assert split_rows % tm == 0 and 0 < split_rows < T
        split_tiles = split_rows // tm
        out_shape = (jax.ShapeDtypeStruct((split_rows, D), F32), jax.ShapeDtypeStruct((T - split_rows, D), F32))
        out_specs = (pl.BlockSpec((tm, D), lambda i, s: (jnp.minimum(i, split_tiles - 1), 0)),
                     pl.BlockSpec((tm, D), lambda i, s: (jnp.maximum(i - split_tiles, 0), 0)))
    grid_spec = pltpu.PrefetchScalarGridSpec(
        num_scalar_prefetch=1,
        grid=(T // tm,),
        in_specs=in_specs,
        out_specs=out_specs,
        scratch_shapes=[pltpu.VMEM((2, TOP_K, tm, D), F32), pltpu.SemaphoreType.DMA((2,))],
    )
    return pl.pallas_call(
        functools.partial(_combine_kernel, tm, with_norm, split_tiles),
        out_shape=out_shape,
        grid_spec=grid_spec,
        compiler_params=_params("arbitrary"),
        name="moe_combine",
    )(*args)


def _rope_tables(Bp, Sp, Bs, Ss, half):
    freqs = ROPE_BASE ** (-jnp.arange(half, dtype=F32) / half)
    pos = jnp.concatenate([jnp.tile(jnp.arange(Sp), Bp), jnp.tile(PAST_LEN + jnp.arange(Ss), Bs)])
    ang = pos.astype(F32)[:, None] * freqs[None, :]
    return jnp.cos(ang), jnp.sin(ang)


def _band_bias(rel_table, C, R, SB):
    H = rel_table.shape[0]
    max_rel = (rel_table.shape[1] - 1) // 2
    L = R + 2 * SB
    t = np.arange(L)
    diag_idx = np.clip(R + SB - 1 - t, -max_rel, max_rel) + max_rel
    diag = jnp.take(rel_table.astype(F32), jnp.asarray(diag_idx, jnp.int32), axis=1)
    flat = jnp.tile(diag, (1, SB))
    toep = flat[:, SB - 1 : SB - 1 + SB * (L - 1)].reshape(H, SB, L - 1)[:, :, : R + SB]
    qc = np.arange(SB)[:, None] // C
    kc = np.arange(R + SB)[None, :] // C
    inband = (kc >= qc) & (kc <= qc + R // C)
    return jnp.where(jnp.asarray(inband)[None], toep * LOG2_E, NEG_INF)


def kernel(x_prompt, x_sample, state_ret, cache_att_k, cache_att_v, norm_mix_w, w_in, ret_gn_w,
           w_ret_o, q_norm_w, k_norm_w, rel_bias, w_att_o, b_gate, w_out, norm_ffn_w, w_router_g,
           b_router_g, w_router_e, b_router_e, w_up, w_down):
    Bp, Sp, D = x_prompt.shape
    Bs, Ss, _ = x_sample.shape
    depth = w_in.shape[0]
    _, _, H_ret, dk, dv = state_ret.shape
    _, _, R, H_att, hd = cache_att_k.shape
    ret_qk, ret_v, att_w = H_ret * dk, H_ret * dv, H_att * hd
    dims = (ret_qk, ret_v, att_w, dk, hd)
    NG, NE = w_router_g.shape[2], w_router_e.shape[2]
    Tp, Ts = Bp * Sp, Bs * Ss
    T = Tp + Ts
    keep = min(R, Sp)
    assert NG + NE <= LANES and R % CHUNK == 0

    blk = 256
    NB = (T * TOP_K + NE * (blk - 1) + blk - 1) // blk
    C_p = 256 if Sp % 256 == 0 else min(CHUNK, Sp)
    G_p = max(1, min(1024, Sp) // C_p)
    C_s = min(CHUNK, Ss)

    x, h = _concat_rmsnorm(x_prompt.reshape(Tp, D), x_sample.reshape(Ts, D), norm_mix_w[0])
    cos, sin = _rope_tables(Bp, Sp, Bs, Ss, dk // 2)
    zero_state = jnp.zeros((Bp, H_ret, dk, dv), F32)
    state_all = state_ret.reshape(depth * Bs, H_ret, dk, dv)
    w_ret_o_bf, w_att_o_bf, w_out_bf = w_ret_o.astype(BF16), w_att_o.astype(BF16), w_out.astype(BF16)
    w_up_bf, w_down_bf = w_up.astype(BF16), w_down.astype(BF16)
    C_att = min(CHUNK, Sp)
    SB_p = 4 * C_att if R % (4 * C_att) == 0 else C_att
    c1 = 2 * ret_qk
    c2 = c1 + 2 * ret_v
    c3 = c2 + 2 * att_w
    c4 = c3 + att_w
    rot_scale = jnp.concatenate([jnp.ones((1, ret_qk), F32), jnp.full((1, ret_qk), float(dk) ** -0.5, F32)], axis=1)

    def mixer_input(l, h):
        qk_gain = jnp.concatenate([jnp.tile(q_norm_w[l] * (float(hd) ** -0.5 * LOG2_E), H_att),
                                   jnp.tile(k_norm_w[l], H_att)]).reshape(1, 2 * att_w)
        rqk = _proj("rotary", h, w_in, l, 0, c1, head=dk, vec=rot_scale, cos=cos, sin=sin)
        rvg = _proj("copy", h, w_in, l, c1, c2 - c1)
        aqk = _proj("norm", h, w_in, l, c2, c3 - c2, head=hd, vec=qk_gain)
        av = _proj("copy", h, w_in, l, c3, c4 - c3)
        sg = _proj("gate", h, w_in, l, c4, 2 * D, vec=b_gate[l].reshape(1, 2 * D))
        return rqk, rvg, aqk, av, sg

    rs_p, k_p, v_p, k_s, v_s = [], [], [], [], []
    rs_s = None
    for l in range(depth):
        wr = jnp.zeros((D, LANES), F32).at[:, :NG].set(w_router_g[l]).at[:, NG : NG + NE].set(w_router_e[l])
        wr_hi = wr.astype(BF16)
        wr_split = jnp.concatenate([wr_hi, (wr - wr_hi.astype(F32)).astype(BF16)], axis=1)
        br = jnp.zeros((1, LANES), F32).at[0, :NG].set(b_router_g[l]).at[0, NG : NG + NE].set(b_router_e[l])
        bias_p = _band_bias(rel_bias[l], C_att, R, SB_p)
        bias_s = _band_bias(rel_bias[l], min(CHUNK, Ss), R, min(CHUNK, Ss))

        rqk, rvg, aqk, av, sg = mixer_input(l, h)

        gated, st_p = _retention(rqk, rvg, zero_state, 0, ret_gn_w[l], None, 0, Bp, Sp, C_p, G_p, dims)
        gated, rs_s = _retention(rqk, rvg, state_all, l * Bs, ret_gn_w[l], gated, Tp, Bs, Ss, C_s, 1, dims,
                                 state_prev=rs_s, state_b0=l * Bs, state_total=depth * Bs)
        att = _attention(aqk, av, None, None, 0, bias_p, None, 0, Bp, Sp, dims)
        att = _attention(aqk, av, cache_att_k, cache_att_v, l, bias_s, att, Tp, Bs, Ss, dims)

        merged = _merge(gated, att, sg, w_ret_o_bf, w_att_o_bf, l)
        x1, h2, ids, gates = _outproj(merged, w_out_bf, l, x, norm_ffn_w[l], wr_split, br, NG, NE)

        slots, block_e, n_used, pad_rows = _slot_tables(ids, NE, blk, NB)
        buf = _dispatch(h2, slots, pad_rows, NB * blk, blk)
        ybuf = _experts(buf, block_e, n_used, w_up_bf, w_down_bf, l, blk)
        if l + 1 < depth:
            x, h = _combine(x1, ybuf, slots, gates, next_norm_w=norm_mix_w[l + 1])
        else:
            y_p, y_s = _combine(x1, ybuf, slots, gates, split_rows=Tp)

        rs_p.append(st_p)

        def newest(a, lo, hi):
            rows_p = jnp.stack([lax.slice(a, ((b + 1) * Sp - keep, lo), ((b + 1) * Sp, hi)) for b in range(Bp)])
            rows_s = lax.slice(a, (Tp, lo), (T, hi))
            return (rows_p.astype(F32).reshape(Bp, keep, H_att, hd), rows_s.astype(F32).reshape(Bs, Ss, H_att, hd))

        kp_l, ks_l = newest(aqk, att_w, 2 * att_w)
        vp_l, vs_l = newest(av, 0, att_w)
        k_p.append(kp_l)
        v_p.append(vp_l)
        k_s.append(ks_l)
        v_s.append(vs_l)

    return (y_p.reshape(Bp, Sp, D), y_s.reshape(Bs, Ss, D), jnp.stack(rs_p),
            rs_s.reshape(depth, Bs, H_ret, dk, dv), jnp.stack(k_p), jnp.stack(v_p),
            jnp.stack(k_s), jnp.stack(v_s))
```

```python
import functools

import jax
import jax.numpy as jnp
import numpy as np
from jax import lax
from jax.experimental import pallas as pl
from jax.experimental.pallas import tpu as pltpu

F32 = jnp.float32
BF16 = jnp.bfloat16

EPS = 1e-6
NEG_INF = -1e30
LOG2_E = 1.4426950408889634
ROPE_BASE = 10000.0
PAST_LEN = 1024
CHUNK = 64
TOP_K = 2

V7X_VMEM_LIMIT_BYTES = 56 * 1024 * 1024
LANES = 128
NT_DIMS = (((1,), (1,)), ((), ()))


def _params(*sem):
    return pltpu.CompilerParams(dimension_semantics=sem, vmem_limit_bytes=V7X_VMEM_LIMIT_BYTES)


def _tile(n, pref):
    t = min(n, pref)
    while n % t:
        t //= 2
    assert t >= 8, (n, pref)
    return t


def _concat_rmsnorm_kernel(split_tiles, xa_ref, xb_ref, w_ref, x_ref, h_ref):
    def emit(src_ref):
        x = src_ref[...]
        x_ref[...] = x
        ms = jnp.mean(x * x, axis=-1, keepdims=True)
        h_ref[...] = ((x * lax.rsqrt(ms + EPS)) * w_ref[...]).astype(h_ref.dtype)

    @pl.when(pl.program_id(0) < split_tiles)
    def _():
        emit(xa_ref)

    @pl.when(pl.program_id(0) >= split_tiles)
    def _():
        emit(xb_ref)


def _concat_rmsnorm(xa, xb, w):
    Ta, D = xa.shape
    Tb = xb.shape[0]
    tm = _tile(Ta, 512)
    while Tb % tm:
        tm //= 2
    na = Ta // tm
    row = pl.BlockSpec((tm, D), lambda i: (i, 0))
    return pl.pallas_call(
        functools.partial(_concat_rmsnorm_kernel, na),
        out_shape=(jax.ShapeDtypeStruct((Ta + Tb, D), F32), jax.ShapeDtypeStruct((Ta + Tb, D), BF16)),
        grid=((Ta + Tb) // tm,),
        in_specs=[
            pl.BlockSpec((tm, D), lambda i: (jnp.minimum(i, na - 1), 0)),
            pl.BlockSpec((tm, D), lambda i: (jnp.maximum(i - na, 0), 0)),
            pl.BlockSpec((1, D), lambda i: (0, 0)),
        ],
        out_specs=(row, row),
        compiler_params=_params("arbitrary"),
        name="concat_rmsnorm",
    )(xa, xb, w.reshape(1, D))


def _proj_kernel(kind, sub, head, h_ref, w_ref, *rest):
    o_ref, w_bf = rest[-2], rest[-1]
    rest = rest[:-1]
    tm, tn = o_ref.shape

    @pl.when(pl.program_id(1) == 0)
    def _():
        w_bf[...] = w_ref[0].astype(BF16)

    for k in range(tm // sub):
        rows = slice(k * sub, (k + 1) * sub)
        acc = jnp.dot(h_ref[rows, :], w_bf[...], preferred_element_type=F32)
        if kind == "rotary":
            cos_ref, sin_ref, scale_ref = rest[:3]
            cos = cos_ref[rows, :]
            sin = sin_ref[rows, :]
            half = head // 2
            for hh in range(tn // head):
                lo = slice(hh * head, hh * head + half)
                hi = slice(hh * head + half, (hh + 1) * head)
                t1 = acc[:, lo]
                t2 = acc[:, hi]
                o_ref[rows, lo] = ((t1 * cos - t2 * sin) * scale_ref[:, lo]).astype(o_ref.dtype)
                o_ref[rows, hi] = ((t1 * sin + t2 * cos) * scale_ref[:, hi]).astype(o_ref.dtype)
        elif kind == "norm":
            gain_ref = rest[0]
            for hh in range(tn // head):
                cols = slice(hh * head, (hh + 1) * head)
                t = acc[:, cols]
                ms = jnp.mean(t * t, axis=-1, keepdims=True)
                o_ref[rows, cols] = ((t * lax.rsqrt(ms + EPS)) * gain_ref[:, cols]).astype(o_ref.dtype)
        elif kind == "gate":
            o_ref[rows, :] = jax.nn.sigmoid(acc + rest[0][...]).astype(o_ref.dtype)
        else:
            o_ref[rows, :] = acc.astype(o_ref.dtype)


def _proj(kind, h, w_all, layer, col0, N, head=None, vec=None, cos=None, sin=None):
    T, D = h.shape
    tm = _tile(T, 1024)
    tn = _tile(N, 1024)
    while col0 % tn:
        tn //= 2
    sub = _tile(tm, 256)
    assert tn % LANES == 0 and (head is None or tn % head == 0)
    jb = col0 // tn
    in_specs = [pl.BlockSpec((tm, D), lambda j, i: (i, 0)),
                pl.BlockSpec((1, D, tn), lambda j, i: (layer, 0, jb + j))]
    args = [h, w_all]
    if kind == "rotary":
        in_specs += [pl.BlockSpec((tm, head // 2), lambda j, i: (i, 0))] * 2
        args += [cos, sin]
    if vec is not None:
        in_specs.append(pl.BlockSpec((1, tn), lambda j, i: (0, j)))
        args.append(vec)
    return pl.pallas_call(
        functools.partial(_proj_kernel, kind, sub, head),
        out_shape=jax.ShapeDtypeStruct((T, N), BF16),
        grid=(N // tn, T // tm),
        in_specs=in_specs,
        out_specs=pl.BlockSpec((tm, tn), lambda j, i: (i, j)),
        scratch_shapes=[pltpu.VMEM((D, tn), BF16)],
        compiler_params=_params("arbitrary", "arbitrary"),
        name="inproj_" + kind,
    )(*args)


def _retention_kernel(C, G, q_ref, k_ref, v_ref, g_ref, s0_ref, intra_ref, cross_ref, sdec_ref,
                      cdec_ref, gnw_ref, *rest):
    o_ref, s_ref = rest[-2], rest[-1]
    n = pl.program_id(2)
    HP = s_ref.shape[1]
    dk = s_ref.shape[2]
    dv = s_ref.shape[3]

    @pl.when(n == 0)
    def _():
        s_ref[...] = s0_ref[...]

    def body(g, carry):
        r0 = pl.multiple_of(g * C, C)
        for hh in range(HP):
            qk_cols = slice(hh * dk, (hh + 1) * dk)
            v_cols = slice(hh * dv, (hh + 1) * dv)
            q = q_ref[pl.ds(r0, C), qk_cols]
            k = k_ref[pl.ds(r0, C), qk_cols]
            v = v_ref[pl.ds(r0, C), v_cols]
            s = s_ref[0, hh]
            scores = lax.dot_general(q, k, NT_DIMS, preferred_element_type=F32) * intra_ref[hh]
            o = jnp.dot(scores.astype(BF16), v, preferred_element_type=F32)
            o = o + jnp.dot(q, s.astype(BF16), preferred_element_type=F32) * cross_ref[hh]
            kd_t = (k.astype(F32) * sdec_ref[hh]).T.astype(BF16)
            s_ref[0, hh] = cdec_ref[hh] * s + jnp.dot(kd_t, v, preferred_element_type=F32)
            mu = jnp.mean(o, axis=-1, keepdims=True)
            d = o - mu
            var = jnp.mean(d * d, axis=-1, keepdims=True)
            y = d * lax.rsqrt(var + EPS) * gnw_ref[hh]
            gate = g_ref[pl.ds(r0, C), v_cols].astype(F32)
            o_ref[pl.ds(r0, C), v_cols] = (gate * jax.nn.sigmoid(gate) * y).astype(o_ref.dtype)
        return carry

    lax.fori_loop(0, G, body, 0)


def _decay_tables(H, C):
    log_gamma = jnp.log1p(-jnp.exp2(-5.0 - jnp.arange(H, dtype=F32)))
    idx = jnp.arange(C, dtype=F32)
    diff = idx[:, None] - idx[None, :]
    intra = jnp.where(diff >= 0, jnp.exp(log_gamma[:, None, None] * jnp.maximum(diff, 0.0)), 0.0)
    cross = jnp.exp(log_gamma[:, None] * (idx + 1.0))[:, :, None]
    sdec = jnp.exp(log_gamma[:, None] * (C - 1.0 - idx))[:, :, None]
    cdec = jnp.exp(log_gamma * C)[:, None, None]
    return intra, cross, sdec, cdec


def _retention(rqk, rvg, s0, s0_b0, gnw, out_prev, row0, B, S, C, G, dims, state_prev=None, state_b0=0,
               state_total=None):
    ret_qk, ret_v, att_w, dk, hd = dims
    state_total = B if state_total is None else state_total
    T = rqk.shape[0]
    H = ret_qk // dk
    dv = ret_v // H
    rows = C * G
    nblk = S // rows
    assert S % rows == 0 and row0 % rows == 0
    blk0 = row0 // rows
    intra, cross, sdec, cdec = _decay_tables(H, C)
    cdec = jnp.broadcast_to(cdec, (H, 1, dv))
    vcol = 0

    HP = 2 if H % 2 == 0 else 1
    kcol, gcol = H // HP, H // HP

    def rowmap(col0):
        return lambda b, h, n: (blk0 + b * nblk + n, col0 + h)

    per_head = lambda b, h, n: (h, 0, 0)
    in_specs = [
        pl.BlockSpec((rows, HP * dk), rowmap(0)),
        pl.BlockSpec((rows, HP * dk), rowmap(kcol)),
        pl.BlockSpec((rows, HP * dv), rowmap(vcol)),
        pl.BlockSpec((rows, HP * dv), rowmap(gcol)),
        pl.BlockSpec((1, HP, dk, dv), lambda b, h, n: (s0_b0 + b, h, 0, 0)),
        pl.BlockSpec((HP, C, C), per_head),
        pl.BlockSpec((HP, C, 1), per_head),
        pl.BlockSpec((HP, C, 1), per_head),
        pl.BlockSpec((HP, 1, dv), per_head),
        pl.BlockSpec((HP, 1, dv), per_head),
    ]
    args = [rqk, rqk, rvg, rvg, s0, intra, cross, sdec, cdec, gnw.reshape(H, 1, dv)]
    aliases = {}
    if out_prev is not None:
        in_specs.append(pl.BlockSpec(memory_space=pl.ANY))
        args.append(out_prev)
        aliases[len(args) - 1] = 0
    if state_prev is not None:
        in_specs.append(pl.BlockSpec(memory_space=pl.ANY))
        args.append(state_prev)
        aliases[len(args) - 1] = 1
    return pl.pallas_call(
        functools.partial(_retention_kernel, C, G),
        out_shape=(jax.ShapeDtypeStruct((T, ret_v), BF16),
                   jax.ShapeDtypeStruct((state_total, H, dk, dv), F32)),
        grid=(B, H // HP, nblk),
        in_specs=in_specs,
        out_specs=(
            pl.BlockSpec((rows, HP * dv), rowmap(0)),
            pl.BlockSpec((1, HP, dk, dv), lambda b, h, n: (state_b0 + b, h, 0, 0)),
        ),
        input_output_aliases=aliases,
        compiler_params=_params("parallel", "parallel", "arbitrary"),
        name="retention",
    )(*args)


def _attention_kernel(C, R, NC, H, hd, mask_first, cache_layer, q_ref, kp_ref, kc_ref, vp_ref, vc_ref,
                      bias_ref, *rest):
    i = pl.program_id(1)
    SB = NC * C
    prev_cache = cache_layer is not None
    prev_off = jnp.where(i > 0, 0.0, NEG_INF) if mask_first else None
    if prev_cache:
        o_ref, kbuf, vbuf, sem = rest[-4:]
        b = pl.program_id(0)

        def head_copies(seq, slot):
            for h in range(H):
                yield pltpu.make_async_copy(kp_ref.at[cache_layer, seq, :, h, :], kbuf.at[slot, h], sem.at[slot])
                yield pltpu.make_async_copy(vp_ref.at[cache_layer, seq, :, h, :], vbuf.at[slot, h], sem.at[slot])

        @pl.when(b == 0)
        def _():
            for cp in head_copies(0, 0):
                cp.start()

        @pl.when(b + 1 < pl.num_programs(0))
        def _():
            for cp in head_copies(b + 1, (b + 1) % 2):
                cp.start()

        slot = b % 2
        for cp in head_copies(b, slot):
            cp.wait()
    else:
        o_ref = rest[-1]

    for sb in range(q_ref.shape[0] // SB):
        r0 = sb * SB
        n_prev = max(R - r0, 0)
        c0 = max(r0 - R, 0)
        for h in range(H):
            lanes = slice(h * hd, (h + 1) * hd)
            q = q_ref[r0 : r0 + SB, lanes]
            kc = kc_ref[c0 : r0 + SB, lanes]
            vc = vc_ref[c0 : r0 + SB, lanes]
            sc = lax.dot_general(q, kc, NT_DIMS, preferred_element_type=F32) + bias_ref[h, :, n_prev:]
            m = jnp.max(sc, axis=-1, keepdims=True)
            if n_prev:
                if prev_cache:
                    kp = kbuf[slot, h, r0:R, :].astype(BF16)
                    vp = vbuf[slot, h, r0:R, :].astype(BF16)
                else:
                    kp = kp_ref[r0:R, lanes]
                    vp = vp_ref[r0:R, lanes]
                sp = lax.dot_general(q, kp, NT_DIMS, preferred_element_type=F32) + bias_ref[h, :, :n_prev]
                if mask_first:
                    sp = sp + prev_off
                m = jnp.maximum(m, jnp.max(sp, axis=-1, keepdims=True))
            pc = jnp.exp2(sc - m)
            l = jnp.sum(pc, axis=-1, keepdims=True)
            o = jnp.dot(pc.astype(BF16), vc, preferred_element_type=F32)
            if n_prev:
                pp = jnp.exp2(sp - m)
                l = l + jnp.sum(pp, axis=-1, keepdims=True)
                o = o + jnp.dot(pp.astype(BF16), vp, preferred_element_type=F32)
            o_ref[r0 : r0 + SB, lanes] = (o * (1.0 / l)).astype(o_ref.dtype)


def _attention(aqk, av, k_past, v_past, past_layer, bias, out_prev, row0, B, S, dims):
    ret_qk, ret_v, att_w, dk, hd = dims
    T = aqk.shape[0]
    H = att_w // hd
    C = min(CHUNK, S)
    SB = bias.shape[1]
    R = bias.shape[2] - SB
    fresh = k_past is None
    rows = R if fresh else S
    assert S % rows == 0 and row0 % rows == 0 and rows % SB == 0 and SB % C == 0
    nblk = S // rows
    blk0 = row0 // rows

    def cur(col):
        return pl.BlockSpec((rows, att_w), lambda b, i: (blk0 + b * nblk + i, col))

    if fresh:
        def prev(col):
            return pl.BlockSpec((rows, att_w), lambda b, i: (blk0 + b * nblk + jnp.maximum(i - 1, 0), col))

        kp_spec, vp_spec = prev(1), prev(0)
        kp_arg, vp_arg = aqk, av
    else:
        assert nblk == 1
        kp_spec = pl.BlockSpec(memory_space=pl.ANY)
        vp_spec = kp_spec
        kp_arg, vp_arg = k_past, v_past
    in_specs = [cur(0), kp_spec, cur(1), vp_spec, cur(0),
                pl.BlockSpec((H, SB, R + SB), lambda b, i: (0, 0, 0), pipeline_mode=pl.Buffered(1))]
    args = [aqk, kp_arg, aqk, vp_arg, av, bias]
    aliases = {}
    if out_prev is not None:
        in_specs.append(pl.BlockSpec(memory_space=pl.ANY))
        args.append(out_prev)
        aliases = {len(args) - 1: 0}
    scratch = []
    if not fresh:
        scratch = [pltpu.VMEM((2, H, R, hd), F32), pltpu.VMEM((2, H, R, hd), F32), pltpu.SemaphoreType.DMA((2,))]
    return pl.pallas_call(
        functools.partial(_attention_kernel, C, R, SB // C, H, hd, fresh, None if fresh else past_layer),
        out_shape=jax.ShapeDtypeStruct((T, att_w), BF16),
        grid=(B, nblk),
        in_specs=in_specs,
        out_specs=pl.BlockSpec((rows, att_w), lambda b, i: (blk0 + b * nblk + i, 0)),
        scratch_shapes=scratch,
        input_output_aliases=aliases,
        compiler_params=_params("parallel" if fresh else "arbitrary", "arbitrary"),
        name="attention",
    )(*args)


def _merge_kernel(ret_ref, wr_ref, att_ref, wa_ref, gr_ref, ga_ref, o_ref):
    y_ret = jnp.dot(ret_ref[...], wr_ref[0], preferred_element_type=F32)
    y_att = jnp.dot(att_ref[...], wa_ref[0], preferred_element_type=F32)
    merged = gr_ref[...].astype(F32) * y_ret + ga_ref[...].astype(F32) * y_att
    o_ref[...] = merged.astype(o_ref.dtype)


def _merge(gated, att, sg, w_ret_o, w_att_o, layer):
    T, ret_v = gated.shape
    att_w = att.shape[1]
    D = w_ret_o.shape[2]
    tm = _tile(T, 256)
    tn = _tile(D, 1024)
    gcol = 0
    return pl.pallas_call(
        _merge_kernel,
        out_shape=jax.ShapeDtypeStruct((T, D), BF16),
        grid=(D // tn, T // tm),
        in_specs=[
            pl.BlockSpec((tm, ret_v), lambda j, i: (i, 0)),
            pl.BlockSpec((1, ret_v, tn), lambda j, i: (layer, 0, j)),
            pl.BlockSpec((tm, att_w), lambda j, i: (i, 0)),
            pl.BlockSpec((1, att_w, tn), lambda j, i: (layer, 0, j)),
            pl.BlockSpec((tm, tn), lambda j, i: (i, gcol + j)),
            pl.BlockSpec((tm, tn), lambda j, i: (i, gcol + D // tn + j)),
        ],
        out_specs=pl.BlockSpec((tm, tn), lambda j, i: (i, j)),
        compiler_params=_params("parallel", "parallel"),
        name="merge",
    )(gated, w_ret_o, att, w_att_o, sg, sg)


def _outproj_kernel(NG, NE, m_ref, w_ref, x_ref, nw_ref, wr_ref, br_ref, x1_ref, h2_ref, ids_ref,
                    gates_ref):
    x1 = x_ref[...] + jnp.dot(m_ref[...], w_ref[0], preferred_element_type=F32)
    x1_ref[...] = x1
    ms = jnp.mean(x1 * x1, axis=-1, keepdims=True)
    h2 = (x1 * lax.rsqrt(ms + EPS)) * nw_ref[...]
    h2_ref[...] = h2
    h_hi = h2.astype(BF16)
    h_lo = (h2 - h_hi.astype(F32)).astype(BF16)
    r_hi = jnp.dot(h_hi, wr_ref[...], preferred_element_type=F32)
    r_lo = jnp.dot(h_lo, wr_ref[:, :LANES], preferred_element_type=F32)
    logits = (r_hi[:, :LANES] + r_hi[:, LANES:]) + r_lo + br_ref[...]
    epg = NE // NG
    lane = lax.broadcasted_iota(jnp.int32, logits.shape, 1).astype(F32)
    big = float(LANES)
    gmask = lane < NG
    gl = jnp.where(gmask, logits, NEG_INF)
    gm = jnp.max(gl, axis=-1, keepdims=True)
    gsel = jnp.min(jnp.where(gl == gm, lane, big), axis=-1, keepdims=True)
    pg = 1.0 / jnp.sum(jnp.where(gmask, jnp.exp(gl - gm), 0.0), axis=-1, keepdims=True)
    lo = NG + gsel * epg
    ingrp = (lane >= lo) & (lane < lo + epg)
    el = jnp.where(ingrp, logits, NEG_INF)
    v1 = jnp.max(el, axis=-1, keepdims=True)
    i1 = jnp.min(jnp.where(el == v1, lane, big), axis=-1, keepdims=True)
    el2 = jnp.where(lane == i1, NEG_INF, el)
    v2 = jnp.max(el2, axis=-1, keepdims=True)
    i2 = jnp.min(jnp.where(el2 == v2, lane, big), axis=-1, keepdims=True)
    t = jnp.exp(v2 - v1)
    den = 1.0 / (1.0 + t)
    g1 = pg * den
    g2 = pg * (t * den)
    ids = jnp.where(lane == 0.0, i1 - NG, jnp.where(lane == 1.0, i2 - NG, 0.0))
    ids_ref[...] = ids.astype(jnp.int32)
    gates_ref[...] = jnp.where(lane == 0.0, g1, jnp.where(lane == 1.0, g2, 0.0))


def _outproj(merged, w_out, layer, x, nw, wr, br, NG, NE):
    T, D = x.shape
    tm = _tile(T, 256)
    row = lambda i: (i, 0)
    const = lambda i: (0, 0)
    return pl.pallas_call(
        functools.partial(_outproj_kernel, NG, NE),
        out_shape=(
            jax.ShapeDtypeStruct((T, D), F32),
            jax.ShapeDtypeStruct((T, D), F32),
            jax.ShapeDtypeStruct((T, LANES), jnp.int32),
            jax.ShapeDtypeStruct((T, LANES), F32),
        ),
        grid=(T // tm,),
        in_specs=[
            pl.BlockSpec((tm, D), row),
            pl.BlockSpec((1, D, D), lambda i: (layer, 0, 0)),
            pl.BlockSpec((tm, D), row),
            pl.BlockSpec((1, D), const),
            pl.BlockSpec((D, 2 * LANES), const),
            pl.BlockSpec((1, LANES), const),
        ],
        out_specs=(
            pl.BlockSpec((tm, D), row),
            pl.BlockSpec((tm, D), row),
            pl.BlockSpec((tm, LANES), row),
            pl.BlockSpec((tm, LANES), row),
        ),
        compiler_params=_params("parallel"),
        name="outproj_router",
    )(merged, w_out, x, nw.reshape(1, D), wr, br)


def _slot_tables(ids, NE, blk, NB):
    e_flat = ids[:, :TOP_K].reshape(-1)
    A = e_flat.shape[0]
    seg = _tile(A, 256)
    onehot = (e_flat[:, None] == jnp.arange(NE, dtype=jnp.int32)[None, :]).reshape(A // seg, seg, NE)
    tril = jnp.tril(jnp.ones((seg, seg), BF16))
    within = jnp.einsum("rc,tce->tre", tril, onehot.astype(BF16), preferred_element_type=F32)
    seg_tot = within[:, -1, :]
    seg_end = jnp.cumsum(seg_tot, axis=0)
    csum = (within + (seg_end - seg_tot)[:, None, :]).astype(jnp.int32)
    counts = seg_end[-1].astype(jnp.int32)
    oh = onehot.astype(jnp.int32)
    rank = jnp.sum(oh * csum, axis=2).reshape(A) - 1
    pcounts = (counts + blk - 1) // blk * blk
    pends = jnp.cumsum(pcounts)
    pstarts = pends - pcounts
    slots = jnp.sum(oh * pstarts[None, None, :], axis=2).reshape(A) + rank
    blk_start = jnp.arange(NB, dtype=jnp.int32) * blk
    block_e = jnp.minimum(jnp.sum((pends[None, :] <= blk_start[:, None]).astype(jnp.int32), axis=1), NE - 1)
    n_used = (pends[-1] // blk).reshape(1)
    pad_rows = jnp.where(pcounts > 0, pends - blk, -1)
    return (slots.astype(jnp.int32), block_e.astype(jnp.int32), n_used.astype(jnp.int32),
            pad_rows.astype(jnp.int32))


def _row_copy(src_ref, src_row, dst_ref, dst_row, sem):
    return pltpu.make_async_copy(src_ref.at[pl.ds(src_row, 1)], dst_ref.at[pl.ds(dst_row, 1)], sem)


def _dispatch_kernel(tm, blk, slots_ref, pad_rows_ref, h_ref, buf_ref, zeros, sem):
    base = pl.program_id(0) * (TOP_K * tm)

    @pl.when(pl.program_id(0) == 0)
    def _():
        zeros[...] = jnp.zeros_like(zeros)
        for e in range(pad_rows_ref.shape[0]):
            @pl.when(pad_rows_ref[e] >= 0)
            def _():
                row0 = pl.multiple_of(pad_rows_ref[e], blk)
                pltpu.make_async_copy(zeros, buf_ref.at[pl.ds(row0, blk)], sem).start()
        for e in range(pad_rows_ref.shape[0]):
            @pl.when(pad_rows_ref[e] >= 0)
            def _():
                pltpu.make_async_copy(zeros, buf_ref.at[pl.ds(0, blk)], sem).wait()

    for r in range(tm):
        for kk in range(TOP_K):
            _row_copy(h_ref, r, buf_ref, slots_ref[base + TOP_K * r + kk], sem).start()
    for kk in range(TOP_K):
        pltpu.make_async_copy(h_ref, buf_ref.at[pl.ds(0, tm)], sem).wait()


def _dispatch(h2, slots, pad_rows, P, blk):
    T, D = h2.shape
    tm = _tile(T, 512)
    grid_spec = pltpu.PrefetchScalarGridSpec(
        num_scalar_prefetch=2,
        grid=(T // tm,),
        in_specs=[pl.BlockSpec((tm, D), lambda i, s, p: (i, 0))],
        out_specs=pl.BlockSpec(memory_space=pl.ANY),
        scratch_shapes=[pltpu.VMEM((blk, D), F32), pltpu.SemaphoreType.DMA(())],
    )
    return pl.pallas_call(
        functools.partial(_dispatch_kernel, tm, blk),
        out_shape=jax.ShapeDtypeStruct((P, D), F32),
        grid_spec=grid_spec,
        compiler_params=_params("arbitrary"),
        name="moe_dispatch",
    )(slots, pad_rows, h2)


def _experts_kernel(block_e_ref, n_used_ref, x_ref, wu_ref, wd_ref, o_ref):
    i = pl.program_id(0)

    @pl.when(i < n_used_ref[0])
    def _():
        x = x_ref[...].astype(BF16)
        up = jnp.dot(x, wu_ref[0, 0], preferred_element_type=F32)
        ff = up.shape[1] // 2
        a = up[:, :ff]
        b = up[:, ff:]
        hmid = (a * jax.nn.sigmoid(a) * b).astype(BF16)
        o_ref[...] = jnp.dot(hmid, wd_ref[0, 0], preferred_element_type=F32)

    @pl.when(i >= n_used_ref[0])
    def _():
        o_ref[...] = jnp.zeros_like(o_ref)


def _experts(buf, block_e, n_used, w_up, w_down, layer, blk):
    P, D = buf.shape
    NB = P // blk
    ff2 = w_up.shape[3]
    grid_spec = pltpu.PrefetchScalarGridSpec(
        num_scalar_prefetch=2,
        grid=(NB,),
        in_specs=[
            pl.BlockSpec((blk, D), lambda i, be, nu: (jnp.where(i < nu[0], i, 0), 0)),
            pl.BlockSpec((1, 1, D, ff2), lambda i, be, nu: (layer, be[i], 0, 0)),
            pl.BlockSpec((1, 1, ff2 // 2, D), lambda i, be, nu: (layer, be[i], 0, 0)),
        ],
        out_specs=pl.BlockSpec((blk, D), lambda i, be, nu: (i, 0)),
    )
    return pl.pallas_call(
        _experts_kernel,
        out_shape=jax.ShapeDtypeStruct((P, D), F32),
        grid_spec=grid_spec,
        compiler_params=_params("arbitrary"),
        name="moe_experts",
    )(block_e, n_used, buf, w_up, w_down)


def _combine_kernel(tm, with_norm, split_tiles, slots_ref, x_ref, gates_ref, y_ref, *rest):
    if with_norm:
        nw_ref, o_ref, h_ref, rows, sem = rest
    elif split_tiles is not None:
        o_ref, o2_ref, rows, sem = rest
    else:
        o_ref, rows, sem = rest
    i = pl.program_id(0)
    n = pl.num_programs(0)

    def fetch(tile, b):
        base = tile * (TOP_K * tm)

        for r in range(tm):
            for kk in range(TOP_K):
                _row_copy(y_ref, slots_ref[base + TOP_K * r + kk], rows.at[b, kk], r, sem.at[b]).start()

    @pl.when(i == 0)
    def _():
        fetch(0, 0)

    @pl.when(i + 1 < n)
    def _():
        fetch(i + 1, (i + 1) % 2)

    b = i % 2
    for kk in range(TOP_K):
        pltpu.make_async_copy(y_ref.at[pl.ds(0, tm)], rows.at[b, kk], sem.at[b]).wait()
    g = gates_ref[...]
    y = g[:, 0:1] * rows[b, 0]
    for kk in range(1, TOP_K):
        y = y + g[:, kk : kk + 1] * rows[b, kk]
    out = x_ref[...] + y
    if split_tiles is None:
        o_ref[...] = out
    else:
        @pl.when(i < split_tiles)
        def _():
            o_ref[...] = out

        @pl.when(i >= split_tiles)
        def _():
            o2_ref[...] = out
    if with_norm:
        ms = jnp.mean(out * out, axis=-1, keepdims=True)
        h_ref[...] = ((out * lax.rsqrt(ms + EPS)) * nw_ref[...]).astype(h_ref.dtype)


def _combine(x1, ybuf, slots, gates, next_norm_w=None, split_rows=None):
    T, D = x1.shape
    tm = _tile(T, 256)
    with_norm = next_norm_w is not None
    split_tiles = None
    row = pl.BlockSpec((tm, D), lambda i, s: (i, 0))
    in_specs = [row, pl.BlockSpec((tm, LANES), lambda i, s: (i, 0)), pl.BlockSpec(memory_space=pl.ANY)]
    args = [slots, x1, gates, ybuf]
    out_shape = jax.ShapeDtypeStruct((T, D), F32)
    out_specs = row
    if with_norm:
        in_specs.append(pl.BlockSpec((1, D), lambda i, s: (0, 0)))
        args.append(next_norm_w.reshape(1, D))
        out_shape = (out_shape, jax.ShapeDtypeStruct((T, D), BF16))
        out_specs = (row, row)
    elif split_rows is not None:
        assert split_rows % tm == 0 and 0 < split_rows < T
        split_tiles = split_rows // tm
        out_shape = (jax.ShapeDtypeStruct((split_rows, D), F32), jax.ShapeDtypeStruct((T - split_rows, D), F32))
        out_specs = (pl.BlockSpec((tm, D), lambda i, s: (jnp.minimum(i, split_tiles - 1), 0)),
                     pl.BlockSpec((tm, D), lambda i, s: (jnp.maximum(i - split_tiles, 0), 0)))
    grid_spec = pltpu.PrefetchScalarGridSpec(
        num_scalar_prefetch=1,
        grid=(T // tm,),
        in_specs=in_specs,
        out_specs=out_specs,
        scratch_shapes=[pltpu.VMEM((2, TOP_K, tm, D), F32), pltpu.SemaphoreType.DMA((2,))],
    )
    return pl.pallas_call(
        functools.partial(_combine_kernel, tm, with_norm, split_tiles),
        out_shape=out_shape,
        grid_spec=grid_spec,
        compiler_params=_params("arbitrary"),
        name="moe_combine",
    )(*args)


def _rope_tables(Bp, Sp, Bs, Ss, half):
    freqs = ROPE_BASE ** (-jnp.arange(half, dtype=F32) / half)
    pos = jnp.concatenate([jnp.tile(jnp.arange(Sp), Bp), jnp.tile(PAST_LEN + jnp.arange(Ss), Bs)])
    ang = pos.astype(F32)[:, None] * freqs[None, :]
    return jnp.cos(ang), jnp.sin(ang)


def _band_bias(rel_table, C, R, SB):
    H = rel_table.shape[0]
    max_rel = (rel_table.shape[1] - 1) // 2
    L = R + 2 * SB
    t = np.arange(L)
    diag_idx = np.clip(R + SB - 1 - t, -max_rel, max_rel) + max_rel
    diag = jnp.take(rel_table.astype(F32), jnp.asarray(diag_idx, jnp.int32), axis=1)
    flat = jnp.tile(diag, (1, SB))
    toep = flat[:, SB - 1 : SB - 1 + SB * (L - 1)].reshape(H, SB, L - 1)[:, :, : R + SB]
    qc = np.arange(SB)[:, None] // C
    kc = np.arange(R + SB)[None, :] // C
    inband = (kc >= qc) & (kc <= qc + R // C)
    return jnp.where(jnp.asarray(inband)[None], toep * LOG2_E, NEG_INF)


def kernel(x_prompt, x_sample, state_ret, cache_att_k, cache_att_v, norm_mix_w, w_in, ret_gn_w,
           w_ret_o, q_norm_w, k_norm_w, rel_bias, w_att_o, b_gate, w_out, norm_ffn_w, w_router_g,
           b_router_g, w_router_e, b_router_e, w_up, w_down):
    Bp, Sp, D = x_prompt.shape
    Bs, Ss, _ = x_sample.shape
    depth = w_in.shape[0]
    _, _, H_ret, dk, dv = state_ret.shape
    _, _, R, H_att, hd = cache_att_k.shape
    ret_qk, ret_v, att_w = H_ret * dk, H_ret * dv, H_att * hd
    dims = (ret_qk, ret_v, att_w, dk, hd)
    NG, NE = w_router_g.shape[2], w_router_e.shape[2]
    Tp, Ts = Bp * Sp, Bs * Ss
    T = Tp + Ts
    keep = min(R, Sp)
    assert NG + NE <= LANES and R % CHUNK == 0

    blk = 256
    NB = (T * TOP_K + NE * (blk - 1) + blk - 1) // blk
    C_p = 256 if Sp % 256 == 0 else min(CHUNK, Sp)
    G_p = max(1, min(1024, Sp) // C_p)
    C_s = min(CHUNK, Ss)

    x, h = _concat_rmsnorm(x_prompt.reshape(Tp, D), x_sample.reshape(Ts, D), norm_mix_w[0])
    cos, sin = _rope_tables(Bp, Sp, Bs, Ss, dk // 2)
    zero_state = jnp.zeros((Bp, H_ret, dk, dv), F32)
    state_all = state_ret.reshape(depth * Bs, H_ret, dk, dv)
    w_ret_o_bf, w_att_o_bf, w_out_bf = w_ret_o.astype(BF16), w_att_o.astype(BF16), w_out.astype(BF16)
    w_up_bf, w_down_bf = w_up.astype(BF16), w_down.astype(BF16)
    C_att = min(CHUNK, Sp)
    SB_p = 4 * C_att if R % (4 * C_att) == 0 else C_att
    c1 = 2 * ret_qk
    c2 = c1 + 2 * ret_v
    c3 = c2 + 2 * att_w
    c4 = c3 + att_w
    rot_scale = jnp.concatenate([jnp.ones((1, ret_qk), F32), jnp.full((1, ret_qk), float(dk) ** -0.5, F32)], axis=1)

    def mixer_input(l, h):
        qk_gain = jnp.concatenate([jnp.tile(q_norm_w[l] * (float(hd) ** -0.5 * LOG2_E), H_att),
                                   jnp.tile(k_norm_w[l], H_att)]).reshape(1, 2 * att_w)
        rqk = _proj("rotary", h, w_in, l, 0, c1, head=dk, vec=rot_scale, cos=cos, sin=sin)
        rvg = _proj("copy", h, w_in, l, c1, c2 - c1)
        aqk = _proj("norm", h, w_in, l, c2, c3 - c2, head=hd, vec=qk_gain)
        av = _proj("copy", h, w_in, l, c3, c4 - c3)
        sg = _proj("gate", h, w_in, l, c4, 2 * D, vec=b_gate[l].reshape(1, 2 * D))
        return rqk, rvg, aqk, av, sg

    rs_p, k_p, v_p, k_s, v_s = [], [], [], [], []
    rs_s = None
    for l in range(depth):
        wr = jnp.zeros((D, LANES), F32).at[:, :NG].set(w_router_g[l]).at[:, NG : NG + NE].set(w_router_e[l])
        wr_hi = wr.astype(BF16)
        wr_split = jnp.concatenate([wr_hi, (wr - wr_hi.astype(F32)).astype(BF16)], axis=1)
        br = jnp.zeros((1, LANES), F32).at[0, :NG].set(b_router_g[l]).at[0, NG : NG + NE].set(b_router_e[l])
        bias_p = _band_bias(rel_bias[l], C_att, R, SB_p)
        bias_s = _band_bias(rel_bias[l], min(CHUNK, Ss), R, min(CHUNK, Ss))

        rqk, rvg, aqk, av, sg = mixer_input(l, h)

        gated, st_p = _retention(rqk, rvg, zero_state, 0, ret_gn_w[l], None, 0, Bp, Sp, C_p, G_p, dims)
        gated, rs_s = _retention(rqk, rvg, state_all, l * Bs, ret_gn_w[l], gated, Tp, Bs, Ss, C_s, 1, dims,
                                 state_prev=rs_s, state_b0=l * Bs, state_total=depth * Bs)
        att = _attention(aqk, av, None, None, 0, bias_p, None, 0, Bp, Sp, dims)
        att = _attention(aqk, av, cache_att_k, cache_att_v, l, bias_s, att, Tp, Bs, Ss, dims)

        merged = _merge(gated, att, sg, w_ret_o_bf, w_att_o_bf, l)
        x1, h2, ids, gates = _outproj(merged, w_out_bf, l, x, norm_ffn_w[l], wr_split, br, NG, NE)

        slots, block_e, n_used, pad_rows = _slot_tables(ids, NE, blk, NB)
        buf = _dispatch(h2, slots, pad_rows, NB * blk, blk)
        ybuf = _experts(buf, block_e, n_used, w_up_bf, w_down_bf, l, blk)
        if l + 1 < depth:
            x, h = _combine(x1, ybuf, slots, gates, next_norm_w=norm_mix_w[l + 1])
        else:
            y_p, y_s = _combine(x1, ybuf, slots, gates, split_rows=Tp)

        rs_p.append(st_p)

        def newest(a, lo, hi):
            rows_p = jnp.stack([lax.slice(a, ((b + 1) * Sp - keep, lo), ((b + 1) * Sp, hi)) for b in range(Bp)])
            rows_s = lax.slice(a, (Tp, lo), (T, hi))
            return (rows_p.astype(F32).reshape(Bp, keep, H_att, hd), rows_s.astype(F32).reshape(Bs, Ss, H_att, hd))

        kp_l, ks_l = newest(aqk, att_w, 2 * att_w)
        vp_l, vs_l = newest(av, 0, att_w)
        k_p.append(kp_l)
        v_p.append(vp_l)
        k_s.append(ks_l)
        v_s.append(vs_l)

    return (y_p.reshape(Bp, Sp, D), y_s.reshape(Bs, Ss, D), jnp.stack(rs_p),
            rs_s.reshape(depth, Bs, H_ret, dk, dv), jnp.stack(k_p), jnp.stack(v_p),
            jnp.stack(k_s), jnp.stack(v_s))
```

```python
import functools

import jax
import jax.numpy as jnp
import numpy as np
from jax import lax
from jax.experimental import pallas as pl
from jax.experimental.pallas import tpu as pltpu

F32 = jnp.float32
BF16 = jnp.bfloat16

EPS = 1e-6
NEG_INF = -1e30
LOG2_E = 1.4426950408889634
ROPE_BASE = 10000.0
PAST_LEN = 1024
CHUNK = 64
TOP_K = 2

V7X_VMEM_LIMIT_BYTES = 56 * 1024 * 1024
LANES = 128
NT_DIMS = (((1,), (1,)), ((), ()))


def _params(*sem):
    return pltpu.CompilerParams(dimension_semantics=sem, vmem_limit_bytes=V7X_VMEM_LIMIT_BYTES)


def _tile(n, pref):
    t = min(n, pref)
    while n % t:
        t //= 2
    assert t >= 8, (n, pref)
    return t


def _concat_rmsnorm_kernel(split_tiles, xa_ref, xb_ref, w_ref, x_ref, h_ref):
    def emit(src_ref):
        x = src_ref[...]
        x_ref[...] = x
        ms = jnp.mean(x * x, axis=-1, keepdims=True)
        h_ref[...] = ((x * lax.rsqrt(ms + EPS)) * w_ref[...]).astype(h_ref.dtype)

    @pl.when(pl.program_id(0) < split_tiles)
    def _():
        emit(xa_ref)

    @pl.when(pl.program_id(0) >= split_tiles)
    def _():
        emit(xb_ref)


def _concat_rmsnorm(xa, xb, w):
    Ta, D = xa.shape
    Tb = xb.shape[0]
    tm = _tile(Ta, 512)
    while Tb % tm:
        tm //= 2
    na = Ta // tm
    row = pl.BlockSpec((tm, D), lambda i: (i, 0))
    return pl.pallas_call(
        functools.partial(_concat_rmsnorm_kernel, na),
        out_shape=(jax.ShapeDtypeStruct((Ta + Tb, D), F32), jax.ShapeDtypeStruct((Ta + Tb, D), BF16)),
        grid=((Ta + Tb) // tm,),
        in_specs=[
            pl.BlockSpec((tm, D), lambda i: (jnp.minimum(i, na - 1), 0)),
            pl.BlockSpec((tm, D), lambda i: (jnp.maximum(i - na, 0), 0)),
            pl.BlockSpec((1, D), lambda i: (0, 0)),
        ],
        out_specs=(row, row),
        compiler_params=_params("arbitrary"),
        name="concat_rmsnorm",
    )(xa, xb, w.reshape(1, D))


def _proj_kernel(kind, sub, head, h_ref, w_ref, *rest):
    o_ref, w_bf = rest[-2], rest[-1]
    rest = rest[:-1]
    tm, tn = o_ref.shape

    @pl.when(pl.program_id(1) == 0)
    def _():
        w_bf[...] = w_ref[0].astype(BF16)

    for k in range(tm // sub):
        rows = slice(k * sub, (k + 1) * sub)
        acc = jnp.dot(h_ref[rows, :], w_bf[...], preferred_element_type=F32)
        if kind == "rotary":
            cos_ref, sin_ref, scale_ref = rest[:3]
            cos = cos_ref[rows, :]
            sin = sin_ref[rows, :]
            half = head // 2
            for hh in range(tn // head):
                lo = slice(hh * head, hh * head + half)
                hi = slice(hh * head + half, (hh + 1) * head)
                t1 = acc[:, lo]
                t2 = acc[:, hi]
                o_ref[rows, lo] = ((t1 * cos - t2 * sin) * scale_ref[:, lo]).astype(o_ref.dtype)
                o_ref[rows, hi] = ((t1 * sin + t2 * cos) * scale_ref[:, hi]).astype(o_ref.dtype)
        elif kind == "norm":
            gain_ref = rest[0]
            for hh in range(tn // head):
                cols = slice(hh * head, (hh + 1) * head)
                t = acc[:, cols]
                ms = jnp.mean(t * t, axis=-1, keepdims=True)
                o_ref[rows, cols] = ((t * lax.rsqrt(ms + EPS)) * gain_ref[:, cols]).astype(o_ref.dtype)
        elif kind == "gate":
            o_ref[rows, :] = jax.nn.sigmoid(acc + rest[0][...]).astype(o_ref.dtype)
        else:
            o_ref[rows, :] = acc.astype(o_ref.dtype)


def _proj(kind, h, w_all, layer, col0, N, head=None, vec=None, cos=None, sin=None):
    T, D = h.shape
    tm = _tile(T, 1024)
    tn = _tile(N, 1024)
    while col0 % tn:
        tn //= 2
    sub = _tile(tm, 256)
    assert tn % LANES == 0 and (head is None or tn % head == 0)
    jb = col0 // tn
    in_specs = [pl.BlockSpec((tm, D), lambda j, i: (i, 0)),
                pl.BlockSpec((1, D, tn), lambda j, i: (layer, 0, jb + j))]
    args = [h, w_all]
    if kind == "rotary":
        in_specs += [pl.BlockSpec((tm, head // 2), lambda j, i: (i, 0))] * 2
        args += [cos, sin]
    if vec is not None:
        in_specs.append(pl.BlockSpec((1, tn), lambda j, i: (0, j)))
        args.append(vec)
    return pl.pallas_call(
        functools.partial(_proj_kernel, kind, sub, head),
        out_shape=jax.ShapeDtypeStruct((T, N), BF16),
        grid=(N // tn, T // tm),
        in_specs=in_specs,
        out_specs=pl.BlockSpec((tm, tn), lambda j, i: (i, j)),
        scratch_shapes=[pltpu.VMEM((D, tn), BF16)],
        compiler_params=_params("arbitrary", "arbitrary"),
        name="inproj_" + kind,
    )(*args)


def _retention_kernel(C, G, q_ref, k_ref, v_ref, g_ref, s0_ref, intra_ref, cross_ref, sdec_ref,
                      cdec_ref, gnw_ref, *rest):
    o_ref, s_ref = rest[-2], rest[-1]
    n = pl.program_id(2)
    HP = s_ref.shape[1]
    dk = s_ref.shape[2]
    dv = s_ref.shape[3]

    @pl.when(n == 0)
    def _():
        s_ref[...] = s0_ref[...]

    def body(g, carry):
        r0 = pl.multiple_of(g * C, C)
        for hh in range(HP):
            qk_cols = slice(hh * dk, (hh + 1) * dk)
            v_cols = slice(hh * dv, (hh + 1) * dv)
            q = q_ref[pl.ds(r0, C), qk_cols]
            k = k_ref[pl.ds(r0, C), qk_cols]
            v = v_ref[pl.ds(r0, C), v_cols]
            s = s_ref[0, hh]
            scores = lax.dot_general(q, k, NT_DIMS, preferred_element_type=F32) * intra_ref[hh]
            o = jnp.dot(scores.astype(BF16), v, preferred_element_type=F32)
            o = o + jnp.dot(q, s.astype(BF16), preferred_element_type=F32) * cross_ref[hh]
            kd_t = (k.astype(F32) * sdec_ref[hh]).T.astype(BF16)
            s_ref[0, hh] = cdec_ref[hh] * s + jnp.dot(kd_t, v, preferred_element_type=F32)
            mu = jnp.mean(o, axis=-1, keepdims=True)
            d = o - mu
            var = jnp.mean(d * d, axis=-1, keepdims=True)
            y = d * lax.rsqrt(var + EPS) * gnw_ref[hh]
            gate = g_ref[pl.ds(r0, C), v_cols].astype(F32)
            o_ref[pl.ds(r0, C), v_cols] = (gate * jax.nn.sigmoid(gate) * y).astype(o_ref.dtype)
        return carry

    lax.fori_loop(0, G, body, 0)


def _decay_tables(H, C):
    log_gamma = jnp.log1p(-jnp.exp2(-5.0 - jnp.arange(H, dtype=F32)))
    idx = jnp.arange(C, dtype=F32)
    diff = idx[:, None] - idx[None, :]
    intra = jnp.where(diff >= 0, jnp.exp(log_gamma[:, None, None] * jnp.maximum(diff, 0.0)), 0.0)
    cross = jnp.exp(log_gamma[:, None] * (idx + 1.0))[:, :, None]
    sdec = jnp.exp(log_gamma[:, None] * (C - 1.0 - idx))[:, :, None]
    cdec = jnp.exp(log_gamma * C)[:, None, None]
    return intra, cross, sdec, cdec


def _retention(rqk, rvg, s0, s0_b0, gnw, out_prev, row0, B, S, C, G, dims, state_prev=None, state_b0=0,
               state_total=None):
    ret_qk, ret_v, att_w, dk, hd = dims
    state_total = B if state_total is None else state_total
    T = rqk.shape[0]
    H = ret_qk // dk
    dv = ret_v // H
    rows = C * G
    nblk = S // rows
    assert S % rows == 0 and row0 % rows == 0
    blk0 = row0 // rows
    intra, cross, sdec, cdec = _decay_tables(H, C)
    cdec = jnp.broadcast_to(cdec, (H, 1, dv))
    vcol = 0

    HP = 2 if H % 2 == 0 else 1
    kcol, gcol = H // HP, H // HP

    def rowmap(col0):
        return lambda b, h, n: (blk0 + b * nblk + n, col0 + h)

    per_head = lambda b, h, n: (h, 0, 0)
    in_specs = [
        pl.BlockSpec((rows, HP * dk), rowmap(0)),
        pl.BlockSpec((rows, HP * dk), rowmap(kcol)),
        pl.BlockSpec((rows, HP * dv), rowmap(vcol)),
        pl.BlockSpec((rows, HP * dv), rowmap(gcol)),
        pl.BlockSpec((1, HP, dk, dv), lambda b, h, n: (s0_b0 + b, h, 0, 0)),
        pl.BlockSpec((HP, C, C), per_head),
        pl.BlockSpec((HP, C, 1), per_head),
        pl.BlockSpec((HP, C, 1), per_head),
        pl.BlockSpec((HP, 1, dv), per_head),
        pl.BlockSpec((HP, 1, dv), per_head),
    ]
    args = [rqk, rqk, rvg, rvg, s0, intra, cross, sdec, cdec, gnw.reshape(H, 1, dv)]
    aliases = {}
    if out_prev is not None:
        in_specs.append(pl.BlockSpec(memory_space=pl.ANY))
        args.append(out_prev)
        aliases[len(args) - 1] = 0
    if state_prev is not None:
        in_specs.append(pl.BlockSpec(memory_space=pl.ANY))
        args.append(state_prev)
        aliases[len(args) - 1] = 1
    return pl.pallas_call(
        functools.partial(_retention_kernel, C, G),
        out_shape=(jax.ShapeDtypeStruct((T, ret_v), BF16),
                   jax.ShapeDtypeStruct((state_total, H, dk, dv), F32)),
        grid=(B, H // HP, nblk),
        in_specs=in_specs,
        out_specs=(
            pl.BlockSpec((rows, HP * dv), rowmap(0)),
            pl.BlockSpec((1, HP, dk, dv), lambda b, h, n: (state_b0 + b, h, 0, 0)),
        ),
        input_output_aliases=aliases,
        compiler_params=_params("parallel", "parallel", "arbitrary"),
        name="retention",
    )(*args)


def _attention_kernel(C, R, NC, H, hd, mask_first, cache_layer, q_ref, kp_ref, kc_ref, vp_ref, vc_ref,
                      bias_ref, *rest):
    i = pl.program_id(1)
    SB = NC * C
    prev_cache = cache_layer is not None
    prev_off = jnp.where(i > 0, 0.0, NEG_INF) if mask_first else None
    if prev_cache:
        o_ref, kbuf, vbuf, sem = rest[-4:]
        b = pl.program_id(0)

        def head_copies(seq, slot):
            for h in range(H):
                yield pltpu.make_async_copy(kp_ref.at[cache_layer, seq, :, h, :], kbuf.at[slot, h], sem.at[slot])
                yield pltpu.make_async_copy(vp_ref.at[cache_layer, seq, :, h, :], vbuf.at[slot, h], sem.at[slot])

        @pl.when(b == 0)
        def _():
            for cp in head_copies(0, 0):
                cp.start()

        @pl.when(b + 1 < pl.num_programs(0))
        def _():
            for cp in head_copies(b + 1, (b + 1) % 2):
                cp.start()

        slot = b % 2
        for cp in head_copies(b, slot):
            cp.wait()
    else:
        o_ref = rest[-1]

    for sb in range(q_ref.shape[0] // SB):
        r0 = sb * SB
        n_prev = max(R - r0, 0)
        c0 = max(r0 - R, 0)
        for h in range(H):
            lanes = slice(h * hd, (h + 1) * hd)
            q = q_ref[r0 : r0 + SB, lanes]
            kc = kc_ref[c0 : r0 + SB, lanes]
            vc = vc_ref[c0 : r0 + SB, lanes]
            sc = lax.dot_general(q, kc, NT_DIMS, preferred_element_type=F32) + bias_ref[h, :, n_prev:]
            m = jnp.max(sc, axis=-1, keepdims=True)
            if n_prev:
                if prev_cache:
                    kp = kbuf[slot, h, r0:R, :].astype(BF16)
                    vp = vbuf[slot, h, r0:R, :].astype(BF16)
                else:
                    kp = kp_ref[r0:R, lanes]
                    vp = vp_ref[r0:R, lanes]
                sp = lax.dot_general(q, kp, NT_DIMS, preferred_element_type=F32) + bias_ref[h, :, :n_prev]
                if mask_first:
                    sp = sp + prev_off
                m = jnp.maximum(m, jnp.max(sp, axis=-1, keepdims=True))
            pc = jnp.exp2(sc - m)
            l = jnp.sum(pc, axis=-1, keepdims=True)
            o = jnp.dot(pc.astype(BF16), vc, preferred_element_type=F32)
            if n_prev:
                pp = jnp.exp2(sp - m)
                l = l + jnp.sum(pp, axis=-1, keepdims=True)
                o = o + jnp.dot(pp.astype(BF16), vp, preferred_element_type=F32)
            o_ref[r0 : r0 + SB, lanes] = (o * (1.0 / l)).astype(o_ref.dtype)


def _attention(aqk, av, k_past, v_past, past_layer, bias, out_prev, row0, B, S, dims):
    ret_qk, ret_v, att_w, dk, hd = dims
    T = aqk.shape[0]
    H = att_w // hd
    C = min(CHUNK, S)
    SB = bias.shape[1]
    R = bias.shape[2] - SB
    fresh = k_past is None
    rows = R if fresh else S
    assert S % rows == 0 and row0 % rows == 0 and rows % SB == 0 and SB % C == 0
    nblk = S // rows
    blk0 = row0 // rows

    def cur(col):
        return pl.BlockSpec((rows, att_w), lambda b, i: (blk0 + b * nblk + i, col))

    if fresh:
        def prev(col):
            return pl.BlockSpec((rows, att_w), lambda b, i: (blk0 + b * nblk + jnp.maximum(i - 1, 0), col))

        kp_spec, vp_spec = prev(1), prev(0)
        kp_arg, vp_arg = aqk, av
    else:
        assert nblk == 1
        kp_spec = pl.BlockSpec(memory_space=pl.ANY)
        vp_spec = kp_spec
        kp_arg, vp_arg = k_past, v_past
    in_specs = [cur(0), kp_spec, cur(1), vp_spec, cur(0),
                pl.BlockSpec((H, SB, R + SB), lambda b, i: (0, 0, 0), pipeline_mode=pl.Buffered(1))]
    args = [aqk, kp_arg, aqk, vp_arg, av, bias]
    aliases = {}
    if out_prev is not None:
        in_specs.append(pl.BlockSpec(memory_space=pl.ANY))
        args.append(out_prev)
        aliases = {len(args) - 1: 0}
    scratch = []
    if not fresh:
        scratch = [pltpu.VMEM((2, H, R, hd), F32), pltpu.VMEM((2, H, R, hd), F32), pltpu.SemaphoreType.DMA((2,))]
    return pl.pallas_call(
        functools.partial(_attention_kernel, C, R, SB // C, H, hd, fresh, None if fresh else past_layer),
        out_shape=jax.ShapeDtypeStruct((T, att_w), BF16),
        grid=(B, nblk),
        in_specs=in_specs,
        out_specs=pl.BlockSpec((rows, att_w), lambda b, i: (blk0 + b * nblk + i, 0)),
        scratch_shapes=scratch,
        input_output_aliases=aliases,
        compiler_params=_params("parallel" if fresh else "arbitrary", "arbitrary"),
        name="attention",
    )(*args)


def _merge_kernel(ret_ref, wr_ref, att_ref, wa_ref, gr_ref, ga_ref, o_ref):
    y_ret = jnp.dot(ret_ref[...], wr_ref[0], preferred_element_type=F32)
    y_att = jnp.dot(att_ref[...], wa_ref[0], preferred_element_type=F32)
    merged = gr_ref[...].astype(F32) * y_ret + ga_ref[...].astype(F32) * y_att
    o_ref[...] = merged.astype(o_ref.dtype)


def _merge(gated, att, sg, w_ret_o, w_att_o, layer):
    T, ret_v = gated.shape
    att_w = att.shape[1]
    D = w_ret_o.shape[2]
    tm = _tile(T, 256)
    tn = _tile(D, 1024)
    gcol = 0
    return pl.pallas_call(
        _merge_kernel,
        out_shape=jax.ShapeDtypeStruct((T, D), BF16),
        grid=(D // tn, T // tm),
        in_specs=[
            pl.BlockSpec((tm, ret_v), lambda j, i: (i, 0)),
            pl.BlockSpec((1, ret_v, tn), lambda j, i: (layer, 0, j)),
            pl.BlockSpec((tm, att_w), lambda j, i: (i, 0)),
            pl.BlockSpec((1, att_w, tn), lambda j, i: (layer, 0, j)),
            pl.BlockSpec((tm, tn), lambda j, i: (i, gcol + j)),
            pl.BlockSpec((tm, tn), lambda j, i: (i, gcol + D // tn + j)),
        ],
        out_specs=pl.BlockSpec((tm, tn), lambda j, i: (i, j)),
        compiler_params=_params("parallel", "parallel"),
        name="merge",
    )(gated, w_ret_o, att, w_att_o, sg, sg)


def _outproj_kernel(NG, NE, m_ref, w_ref, x_ref, nw_ref, wr_ref, br_ref, x1_ref, h2_ref, ids_ref,
                    gates_ref):
    x1 = x_ref[...] + jnp.dot(m_ref[...], w_ref[0], preferred_element_type=F32)
    x1_ref[...] = x1
    ms = jnp.mean(x1 * x1, axis=-1, keepdims=True)
    h2 = (x1 * lax.rsqrt(ms + EPS)) * nw_ref[...]
    h2_ref[...] = h2
    h_hi = h2.astype(BF16)
    h_lo = (h2 - h_hi.astype(F32)).astype(BF16)
    r_hi = jnp.dot(h_hi, wr_ref[...], preferred_element_type=F32)
    r_lo = jnp.dot(h_lo, wr_ref[:, :LANES], preferred_element_type=F32)
    logits = (r_hi[:, :LANES] + r_hi[:, LANES:]) + r_lo + br_ref[...]
    epg = NE // NG
    lane = lax.broadcasted_iota(jnp.int32, logits.shape, 1).astype(F32)
    big = float(LANES)
    gmask = lane < NG
    gl = jnp.where(gmask, logits, NEG_INF)
    gm = jnp.max(gl, axis=-1, keepdims=True)
    gsel = jnp.min(jnp.where(gl == gm, lane, big), axis=-1, keepdims=True)
    pg = 1.0 / jnp.sum(jnp.where(gmask, jnp.exp(gl - gm), 0.0), axis=-1, keepdims=True)
    lo = NG + gsel * epg
    ingrp = (lane >= lo) & (lane < lo + epg)
    el = jnp.where(ingrp, logits, NEG_INF)
    v1 = jnp.max(el, axis=-1, keepdims=True)
    i1 = jnp.min(jnp.where(el == v1, lane, big), axis=-1, keepdims=True)
    el2 = jnp.where(lane == i1, NEG_INF, el)
    v2 = jnp.max(el2, axis=-1, keepdims=True)
    i2 = jnp.min(jnp.where(el2 == v2, lane, big), axis=-1, keepdims=True)
    t = jnp.exp(v2 - v1)
    den = 1.0 / (1.0 + t)
    g1 = pg * den
    g2 = pg * (t * den)
    ids = jnp.where(lane == 0.0, i1 - NG, jnp.where(lane == 1.0, i2 - NG, 0.0))
    ids_ref[...] = ids.astype(jnp.int32)
    gates_ref[...] = jnp.where(lane == 0.0, g1, jnp.where(lane == 1.0, g2, 0.0))


def _outproj(merged, w_out, layer, x, nw, wr, br, NG, NE):
    T, D = x.shape
    tm = _tile(T, 256)
    row = lambda i: (i, 0)
    const = lambda i: (0, 0)
    return pl.pallas_call(
        functools.partial(_outproj_kernel, NG, NE),
        out_shape=(
            jax.ShapeDtypeStruct((T, D), F32),
            jax.ShapeDtypeStruct((T, D), F32),
            jax.ShapeDtypeStruct((T, LANES), jnp.int32),
            jax.ShapeDtypeStruct((T, LANES), F32),
        ),
        grid=(T // tm,),
        in_specs=[
            pl.BlockSpec((tm, D), row),
            pl.BlockSpec((1, D, D), lambda i: (layer, 0, 0)),
            pl.BlockSpec((tm, D), row),
            pl.BlockSpec((1, D), const),
            pl.BlockSpec((D, 2 * LANES), const),
            pl.BlockSpec((1, LANES), const),
        ],
        out_specs=(
            pl.BlockSpec((tm, D), row),
            pl.BlockSpec((tm, D), row),
            pl.BlockSpec((tm, LANES), row),
            pl.BlockSpec((tm, LANES), row),
        ),
        compiler_params=_params("parallel"),
        name="outproj_router",
    )(merged, w_out, x, nw.reshape(1, D), wr, br)


def _slot_tables(ids, NE, blk, NB):
    e_flat = ids[:, :TOP_K].reshape(-1)
    A = e_flat.shape[0]
    seg = _tile(A, 256)
    onehot = (e_flat[:, None] == jnp.arange(NE, dtype=jnp.int32)[None, :]).reshape(A // seg, seg, NE)
    tril = jnp.tril(jnp.ones((seg, seg), BF16))
    within = jnp.einsum("rc,tce->tre", tril, onehot.astype(BF16), preferred_element_type=F32)
    seg_tot = within[:, -1, :]
    seg_end = jnp.cumsum(seg_tot, axis=0)
    csum = (within + (seg_end - seg_tot)[:, None, :]).astype(jnp.int32)
    counts = seg_end[-1].astype(jnp.int32)
    oh = onehot.astype(jnp.int32)
    rank = jnp.sum(oh * csum, axis=2).reshape(A) - 1
    pcounts = (counts + blk - 1) // blk * blk
    pends = jnp.cumsum(pcounts)
    pstarts = pends - pcounts
    slots = jnp.sum(oh * pstarts[None, None, :], axis=2).reshape(A) + rank
    blk_start = jnp.arange(NB, dtype=jnp.int32) * blk
    block_e = jnp.minimum(jnp.sum((pends[None, :] <= blk_start[:, None]).astype(jnp.int32), axis=1), NE - 1)
    n_used = (pends[-1] // blk).reshape(1)
    pad_rows = jnp.where(pcounts > 0, pends - blk, -1)
    return (slots.astype(jnp.int32), block_e.astype(jnp.int32), n_used.astype(jnp.int32),
            pad_rows.astype(jnp.int32))


def _row_copy(src_ref, src_row, dst_ref, dst_row, sem):
    return pltpu.make_async_copy(src_ref.at[pl.ds(src_row, 1)], dst_ref.at[pl.ds(dst_row, 1)], sem)


def _dispatch_kernel(tm, blk, slots_ref, pad_rows_ref, h_ref, buf_ref, zeros, sem):
    base = pl.program_id(0) * (TOP_K * tm)

    @pl.when(pl.program_id(0) == 0)
    def _():
        zeros[...] = jnp.zeros_like(zeros)
        for e in range(pad_rows_ref.shape[0]):
            @pl.when(pad_rows_ref[e] >= 0)
            def _():
                row0 = pl.multiple_of(pad_rows_ref[e], blk)
                pltpu.make_async_copy(zeros, buf_ref.at[pl.ds(row0, blk)], sem).start()
        for e in range(pad_rows_ref.shape[0]):
            @pl.when(pad_rows_ref[e] >= 0)
            def _():
                pltpu.make_async_copy(zeros, buf_ref.at[pl.ds(0, blk)], sem).wait()

    for r in range(tm):
        for kk in range(TOP_K):
            _row_copy(h_ref, r, buf_ref, slots_ref[base + TOP_K * r + kk], sem).start(priority=r % 2)
    for kk in range(TOP_K):
        pltpu.make_async_copy(h_ref, buf_ref.at[pl.ds(0, tm)], sem).wait()


def _dispatch(h2, slots, pad_rows, P, blk):
    T, D = h2.shape
    tm = _tile(T, 512)
    grid_spec = pltpu.PrefetchScalarGridSpec(
        num_scalar_prefetch=2,
        grid=(T // tm,),
        in_specs=[pl.BlockSpec((tm, D), lambda i, s, p: (i, 0))],
        out_specs=pl.BlockSpec(memory_space=pl.ANY),
        scratch_shapes=[pltpu.VMEM((blk, D), F32), pltpu.SemaphoreType.DMA(())],
    )
    return pl.pallas_call(
        functools.partial(_dispatch_kernel, tm, blk),
        out_shape=jax.ShapeDtypeStruct((P, D), F32),
        grid_spec=grid_spec,
        compiler_params=_params("arbitrary"),
        name="moe_dispatch",
    )(slots, pad_rows, h2)


def _experts_kernel(block_e_ref, n_used_ref, x_ref, wu_ref, wd_ref, o_ref):
    i = pl.program_id(0)

    @pl.when(i < n_used_ref[0])
    def _():
        x = x_ref[...].astype(BF16)
        up = jnp.dot(x, wu_ref[0, 0], preferred_element_type=F32)
        ff = up.shape[1] // 2
        a = up[:, :ff]
        b = up[:, ff:]
        hmid = (a * jax.nn.sigmoid(a) * b).astype(BF16)
        o_ref[...] = jnp.dot(hmid, wd_ref[0, 0], preferred_element_type=F32)

    @pl.when(i >= n_used_ref[0])
    def _():
        o_ref[...] = jnp.zeros_like(o_ref)


def _experts(buf, block_e, n_used, w_up, w_down, layer, blk):
    P, D = buf.shape
    NB = P // blk
    ff2 = w_up.shape[3]
    grid_spec = pltpu.PrefetchScalarGridSpec(
        num_scalar_prefetch=2,
        grid=(NB,),
        in_specs=[
            pl.BlockSpec((blk, D), lambda i, be, nu: (jnp.where(i < nu[0], i, 0), 0)),
            pl.BlockSpec((1, 1, D, ff2), lambda i, be, nu: (layer, be[i], 0, 0)),
            pl.BlockSpec((1, 1, ff2 // 2, D), lambda i, be, nu: (layer, be[i], 0, 0)),
        ],
        out_specs=pl.BlockSpec((blk, D), lambda i, be, nu: (i, 0)),
    )
    return pl.pallas_call(
        _experts_kernel,
        out_shape=jax.ShapeDtypeStruct((P, D), F32),
        grid_spec=grid_spec,
        compiler_params=_params("arbitrary"),
        name="moe_experts",
    )(block_e, n_used, buf, w_up, w_down)


def _combine_kernel(tm, with_norm, split_tiles, slots_ref, x_ref, gates_ref, y_ref, *rest):
    if with_norm:
        nw_ref, o_ref, h_ref, rows, sem = rest
    elif split_tiles is not None:
        o_ref, o2_ref, rows, sem = rest
    else:
        o_ref, rows, sem = rest
    i = pl.program_id(0)
    n = pl.num_programs(0)

    def fetch(tile, b):
        base = tile * (TOP_K * tm)

        for r in range(tm):
            for kk in range(TOP_K):
                _row_copy(y_ref, slots_ref[base + TOP_K * r + kk], rows.at[b, kk], r, sem.at[b]).start()

    @pl.when(i == 0)
    def _():
        fetch(0, 0)

    @pl.when(i + 1 < n)
    def _():
        fetch(i + 1, (i + 1) % 2)

    b = i % 2
    for kk in range(TOP_K):
        pltpu.make_async_copy(y_ref.at[pl.ds(0, tm)], rows.at[b, kk], sem.at[b]).wait()
    g = gates_ref[...]
    y = g[:, 0:1] * rows[b, 0]
    for kk in range(1, TOP_K):
        y = y + g[:, kk : kk + 1] * rows[b, kk]
    out = x_ref[...] + y
    if split_tiles is None:
        o_ref[...] = out
    else:
        @pl.when(i < split_tiles)
        def _():
            o_ref[...] = out

        @pl.when(i >= split_tiles)
        def _():
            o2_ref[...] = out
    if with_norm:
        ms = jnp.mean(out * out, axis=-1, keepdims=True)
        h_ref[...] = ((out * lax.rsqrt(ms + EPS)) * nw_ref[...]).astype(h_ref.dtype)


def _combine(x1, ybuf, slots, gates, next_norm_w=None, split_rows=None):
    T, D = x1.shape
    tm = _tile(T, 256)
    with_norm = next_norm_w is not None
    split_tiles = None
    row = pl.BlockSpec((tm, D), lambda i, s: (i, 0))
    in_specs = [row, pl.BlockSpec((tm, LANES), lambda i, s: (i, 0)), pl.BlockSpec(memory_space=pl.ANY)]
    args = [slots, x1, gates, ybuf]
    out_shape = jax.ShapeDtypeStruct((T, D), F32)
    out_specs = row
    if with_norm:
        in_specs.append(pl.BlockSpec((1, D), lambda i, s: (0, 0)))
        args.append(next_norm_w.reshape(1, D))
        out_shape = (out_shape, jax.ShapeDtypeStruct((T, D), BF16))
        out_specs = (row, row)
    elif split_rows is not None:
        assert split_rows % tm == 0 and 0 < split_rows < T
        split_tiles = split_rows // tm
        out_shape = (jax.ShapeDtypeStruct((split_rows, D), F32), jax.ShapeDtypeStruct((T - split_rows, D), F32))
        out_specs = (pl.BlockSpec((tm, D), lambda i, s: (jnp.minimum(i, split_tiles - 1), 0)),
                     pl.BlockSpec((tm, D), lambda i, s: (jnp.maximum(i - split_tiles, 0), 0)))
    grid_spec = pltpu.PrefetchScalarGridSpec(
        num_scalar_prefetch=1,
        grid=(T // tm,),
        in_specs=in_specs,
        out_specs=out_specs,
        scratch_shapes=[pltpu.VMEM((2, TOP_K, tm, D), F32), pltpu.SemaphoreType.DMA((2,))],
    )
    return pl.pallas_call(
        functools.partial(_combine_kernel, tm, with_norm, split_tiles),
        out_shape=out_shape,
        grid_spec=grid_spec,
        compiler_params=_params("arbitrary"),
        name="moe_combine",
    )(*args)


def _rope_tables(Bp, Sp, Bs, Ss, half):
    freqs = ROPE_BASE ** (-jnp.arange(half, dtype=F32) / half)
    pos = jnp.concatenate([jnp.tile(jnp.arange(Sp), Bp), jnp.tile(PAST_LEN + jnp.arange(Ss), Bs)])
    ang = pos.astype(F32)[:, None] * freqs[None, :]
    return jnp.cos(ang), jnp.sin(ang)


def _band_bias(rel_table, C, R, SB):
    H = rel_table.shape[0]
    max_rel = (rel_table.shape[1] - 1) // 2
    L = R + 2 * SB
    t = np.arange(L)
    diag_idx = np.clip(R + SB - 1 - t, -max_rel, max_rel) + max_rel
    diag = jnp.take(rel_table.astype(F32), jnp.asarray(diag_idx, jnp.int32), axis=1)
    flat = jnp.tile(diag, (1, SB))
    toep = flat[:, SB - 1 : SB - 1 + SB * (L - 1)].reshape(H, SB, L - 1)[:, :, : R + SB]
    qc = np.arange(SB)[:, None] // C
    kc = np.arange(R + SB)[None, :] // C
    inband = (kc >= qc) & (kc <= qc + R // C)
    return jnp.where(jnp.asarray(inband)[None], toep * LOG2_E, NEG_INF)


def kernel(x_prompt, x_sample, state_ret, cache_att_k, cache_att_v, norm_mix_w, w_in, ret_gn_w,
           w_ret_o, q_norm_w, k_norm_w, rel_bias, w_att_o, b_gate, w_out, norm_ffn_w, w_router_g,
           b_router_g, w_router_e, b_router_e, w_up, w_down):
    Bp, Sp, D = x_prompt.shape
    Bs, Ss, _ = x_sample.shape
    depth = w_in.shape[0]
    _, _, H_ret, dk, dv = state_ret.shape
    _, _, R, H_att, hd = cache_att_k.shape
    ret_qk, ret_v, att_w = H_ret * dk, H_ret * dv, H_att * hd
    dims = (ret_qk, ret_v, att_w, dk, hd)
    NG, NE = w_router_g.shape[2], w_router_e.shape[2]
    Tp, Ts = Bp * Sp, Bs * Ss
    T = Tp + Ts
    keep = min(R, Sp)
    assert NG + NE <= LANES and R % CHUNK == 0

    blk = 256
    NB = (T * TOP_K + NE * (blk - 1) + blk - 1) // blk
    C_p = 256 if Sp % 256 == 0 else min(CHUNK, Sp)
    G_p = max(1, min(1024, Sp) // C_p)
    C_s = min(CHUNK, Ss)

    x, h = _concat_rmsnorm(x_prompt.reshape(Tp, D), x_sample.reshape(Ts, D), norm_mix_w[0])
    cos, sin = _rope_tables(Bp, Sp, Bs, Ss, dk // 2)
    zero_state = jnp.zeros((Bp, H_ret, dk, dv), F32)
    state_all = state_ret.reshape(depth * Bs, H_ret, dk, dv)
    w_ret_o_bf, w_att_o_bf, w_out_bf = w_ret_o.astype(BF16), w_att_o.astype(BF16), w_out.astype(BF16)
    w_up_bf, w_down_bf = w_up.astype(BF16), w_down.astype(BF16)
    C_att = min(CHUNK, Sp)
    SB_p = 4 * C_att if R % (4 * C_att) == 0 else C_att
    c1 = 2 * ret_qk
    c2 = c1 + 2 * ret_v
    c3 = c2 + 2 * att_w
    c4 = c3 + att_w
    rot_scale = jnp.concatenate([jnp.ones((1, ret_qk), F32), jnp.full((1, ret_qk), float(dk) ** -0.5, F32)], axis=1)

    def mixer_input(l, h):
        qk_gain = jnp.concatenate([jnp.tile(q_norm_w[l] * (float(hd) ** -0.5 * LOG2_E), H_att),
                                   jnp.tile(k_norm_w[l], H_att)]).reshape(1, 2 * att_w)
        rqk = _proj("rotary", h, w_in, l, 0, c1, head=dk, vec=rot_scale, cos=cos, sin=sin)
        rvg = _proj("copy", h, w_in, l, c1, c2 - c1)
        aqk = _proj("norm", h, w_in, l, c2, c3 - c2, head=hd, vec=qk_gain)
        av = _proj("copy", h, w_in, l, c3, c4 - c3)
        sg = _proj("gate", h, w_in, l, c4, 2 * D, vec=b_gate[l].reshape(1, 2 * D))
        return rqk, rvg, aqk, av, sg

    rs_p, k_p, v_p, k_s, v_s = [], [], [], [], []
    rs_s = None
    for l in range(depth):
        wr = jnp.zeros((D, LANES), F32).at[:, :NG].set(w_router_g[l]).at[:, NG : NG + NE].set(w_router_e[l])
        wr_hi = wr.astype(BF16)
        wr_split = jnp.concatenate([wr_hi, (wr - wr_hi.astype(F32)).astype(BF16)], axis=1)
        br = jnp.zeros((1, LANES), F32).at[0, :NG].set(b_router_g[l]).at[0, NG : NG + NE].set(b_router_e[l])
        bias_p = _band_bias(rel_bias[l], C_att, R, SB_p)
        bias_s = _band_bias(rel_bias[l], min(CHUNK, Ss), R, min(CHUNK, Ss))

        rqk, rvg, aqk, av, sg = mixer_input(l, h)

        gated, st_p = _retention(rqk, rvg, zero_state, 0, ret_gn_w[l], None, 0, Bp, Sp, C_p, G_p, dims)
        gated, rs_s = _retention(rqk, rvg, state_all, l * Bs, ret_gn_w[l], gated, Tp, Bs, Ss, C_s, 1, dims,
                                 state_prev=rs_s, state_b0=l * Bs, state_total=depth * Bs)
        att = _attention(aqk, av, None, None, 0, bias_p, None, 0, Bp, Sp, dims)
        att = _attention(aqk, av, cache_att_k, cache_att_v, l, bias_s, att, Tp, Bs, Ss, dims)

        merged = _merge(gated, att, sg, w_ret_o_bf, w_att_o_bf, l)
        x1, h2, ids, gates = _outproj(merged, w_out_bf, l, x, norm_ffn_w[l], wr_split, br, NG, NE)

        slots, block_e, n_used, pad_rows = _slot_tables(ids, NE, blk, NB)
        buf = _dispatch(h2, slots, pad_rows, NB * blk, blk)
        ybuf = _experts(buf, block_e, n_used, w_up_bf, w_down_bf, l, blk)
        if l + 1 < depth:
            x, h = _combine(x1, ybuf, slots, gates, next_norm_w=norm_mix_w[l + 1])
        else:
            y_p, y_s = _combine(x1, ybuf, slots, gates, split_rows=Tp)

        rs_p.append(st_p)

        def newest(a, lo, hi):
            rows_p = jnp.stack([lax.slice(a, ((b + 1) * Sp - keep, lo), ((b + 1) * Sp, hi)) for b in range(Bp)])
            rows_s = lax.slice(a, (Tp, lo), (T, hi))
            return (rows_p.astype(F32).reshape(Bp, keep, H_att, hd), rows_s.astype(F32).reshape(Bs, Ss, H_att, hd))

        kp_l, ks_l = newest(aqk, att_w, 2 * att_w)
        vp_l, vs_l = newest(av, 0, att_w)
        k_p.append(kp_l)
        v_p.append(vp_l)
        k_s.append(ks_l)
        v_s.append(vs_l)

    return (y_p.reshape(Bp, Sp, D), y_s.reshape(Bs, Ss, D), jnp.stack(rs_p),
            rs_s.reshape(depth, Bs, H_ret, dk, dv), jnp.stack(k_p), jnp.stack(v_p),
            jnp.stack(k_s), jnp.stack(v_s))
```
